```python
import math
import jax, jax.numpy as jnp
from jax import lax
import numpy as np

D_MODEL = 1024
BATCH = 16
SEQ = 2048
DEPTH = 4
DEC_BATCH = 2
DEC_SEQ = 16384
PAST_LEN = 128

HEAD_DIM = 64
ATTN_WIDTH = 3 * D_MODEL // 8
ATTN_HEADS = ATTN_WIDTH // HEAD_DIM
ATTN_KV_HEADS = ATTN_HEADS // 3
KV_WIDTH = ATTN_KV_HEADS * HEAD_DIM
ATTN_WINDOW = 128
ATTN_BLOCK = 128
ROPE_THETA = 500000.0
ROPE_DIM = HEAD_DIM // 4
SSM_WIDTH = D_MODEL // 4
SSM_GROUP = 16
SSM_GROUPS = SSM_WIDTH // SSM_GROUP
SSM_STATE = 64
RET_WIDTH = 3 * D_MODEL // 8
RET_HEADS = RET_WIDTH // HEAD_DIM
RET_CHUNK = 128
RET_THETA = 10000.0
MIX_WIDTH = ATTN_WIDTH + SSM_WIDTH + RET_WIDTH
IN_SPLITS = (ATTN_WIDTH, KV_WIDTH, KV_WIDTH, SSM_WIDTH, RET_WIDTH, RET_WIDTH, RET_WIDTH, RET_WIDTH)
IN_WIDTH = sum(IN_SPLITS)
D_FF = 2816
FFN_CONV = 3
DN_ALPHA = (2 * DEPTH) ** 0.25
DN_BETA = (8 * DEPTH) ** -0.25
EPS = 1e-5

kernel_name = "hymba_style_bidir_encoder"

F32 = jnp.float32


def layer_norm(x, g, b):
    xf = x.astype(F32)
    mu = jnp.mean(xf, -1, keepdims=True)
    xc = xf - mu
    var = jnp.mean(xc * xc, -1, keepdims=True)
    return (xc * lax.rsqrt(var + EPS) * g.astype(F32) + b.astype(F32)).astype(x.dtype)


def rms_norm(x, g):
    xf = x.astype(F32)
    return xf * lax.rsqrt(jnp.mean(xf * xf, -1, keepdims=True) + EPS) * g.astype(F32)


def rotate(x, inv_freq):
    r = 2 * inv_freq.shape[0]
    L = x.shape[1]
    ang = jnp.arange(L, dtype=F32)[:, None] * inv_freq[None, :]
    cos = jnp.cos(ang)[None, :, None, :].astype(x.dtype)
    sin = jnp.sin(ang)[None, :, None, :].astype(x.dtype)
    x1 = x[..., : r // 2]
    x2 = x[..., r // 2: r]
    return jnp.concatenate([x1 * cos - x2 * sin, x1 * sin + x2 * cos, x[..., r:]], axis=-1)


def windowed_attention(q, k, v, sink):
    Bsz, L, H, hd = q.shape
    KV = k.shape[2]
    G = H // KV
    BLK = ATTN_BLOCK
    nb = L // BLK
    qb = q.reshape(Bsz, nb, BLK, KV, G, hd)
    pad = ((0, 0), (BLK, BLK), (0, 0), (0, 0))
    kp = jnp.pad(k, pad).reshape(Bsz, nb + 2, BLK, KV, hd)
    vp = jnp.pad(v, pad).reshape(Bsz, nb + 2, BLK, KV, hd)
    kw = jnp.concatenate([kp[:, :-2], kp[:, 1:-1], kp[:, 2:]], axis=2)
    vw = jnp.concatenate([vp[:, :-2], vp[:, 1:-1], vp[:, 2:]], axis=2)
    s = jnp.einsum('bnqkgd,bnskd->bnkgqs', qb, kw, preferred_element_type=F32) * (hd ** -0.5)
    qi = jnp.arange(BLK)
    kj = jnp.arange(3 * BLK) - BLK
    band = jnp.abs(kj[None, :] - qi[:, None]) <= ATTN_WINDOW
    kabs = jnp.arange(nb)[:, None] * BLK + kj[None, :]
    inside = (kabs >= 0) & (kabs < L)
    mask = band[None, :, :] & inside[:, None, :]
    s = jnp.where(mask[None, :, None, None], s, -1e30)
    sk = sink.astype(F32).reshape(KV, G)[None, None, :, :, None, None]
    m = jnp.maximum(jnp.max(s, -1, keepdims=True), sk)
    p = jnp.exp(s - m)
    p = p / (jnp.sum(p, -1, keepdims=True) + jnp.exp(sk - m))
    o = jnp.einsum('bnkgqs,bnskd->bnqkgd', p.astype(v.dtype), vw)
    return o.reshape(Bsz, L, H, hd)


def _linear_combine(left, right):
    a1, b1 = left
    a2, b2 = right
    return a1 * a2, a2 * b1 + b2


def s5_bidirectional(u, lam_re, lam_im, log_dt, b_re, b_im, c_re, c_im, d_skip):
    Bsz, L, _ = u.shape
    uf = u.astype(F32).reshape(Bsz, L, SSM_GROUPS, SSM_GROUP)
    uc = uf.astype(jnp.complex64)
    y = d_skip.astype(F32).reshape(SSM_GROUPS, SSM_GROUP) * uf
    for z in range(2):
        lam = lax.complex(lam_re[z].astype(F32), lam_im[z].astype(F32))
        dt = jnp.exp(log_dt[z].astype(F32))[:, None]
        lam_bar = jnp.exp(lam * dt)
        b_bar = ((lam_bar - 1.0) / lam)[:, :, None] * lax.complex(b_re[z].astype(F32), b_im[z].astype(F32))
        bu = jnp.einsum('blgh,gph->blgp', uc, b_bar)
        a = jnp.broadcast_to(lam_bar, bu.shape)
        _, states = lax.associative_scan(_linear_combine, (a, bu), axis=1, reverse=(z == 1))
        c = lax.complex(c_re[z].astype(F32), c_im[z].astype(F32))
        y = y + jnp.real(jnp.einsum('blgp,ghp->blgh', states, c))
    return y.reshape(Bsz, L, SSM_WIDTH)


def retention(q, k, v):
    Bsz, L, H, d = q.shape
    C = RET_CHUNK
    nc = L // C
    log_g = jnp.log(1.0 - 2.0 ** (-5.0 - jnp.arange(H, dtype=F32)))
    idx = jnp.arange(C, dtype=F32)
    inv_freq = 1.0 / (RET_THETA ** jnp.linspace(0.0, 1.0, d // 2, dtype=F32))
    q = rotate(q.astype(F32), inv_freq)
    k = rotate(k.astype(F32), inv_freq) * (d ** -0.5)
    qc = q.reshape(Bsz, nc, C, H, d)
    kc = k.reshape(Bsz, nc, C, H, d)
    vc = v.astype(F32).reshape(Bsz, nc, C, H, d)
    decay = jnp.exp(log_g[:, None, None] * jnp.abs(idx[:, None] - idx[None, :]))
    s = jnp.einsum('bnihd,bnjhd->bnhij', qc, kc) * decay
    out = jnp.einsum('bnhij,bnjhe->bnihe', s, vc)
    k_fwd = kc * jnp.exp(log_g[None, :] * (C - 1.0 - idx)[:, None])[:, :, None]
    k_bwd = kc * jnp.exp(log_g[None, :] * idx[:, None])[:, :, None]
    r_fwd = jnp.einsum('bnjhd,bnjhe->nbhde', k_fwd, vc)
    r_bwd = jnp.einsum('bnjhd,bnjhe->nbhde', k_bwd, vc)
    g_chunk = jnp.exp(log_g * C)[None, :, None, None]

    def step(state, r):
        return g_chunk * state + r, state

    init = jnp.zeros((Bsz, H, d, d), F32)
    _, s_fwd = lax.scan(step, init, r_fwd)
    _, s_bwd = lax.scan(step, init, r_bwd, reverse=True)
    q_fwd = qc * jnp.exp(log_g[None, :] * (idx + 1.0)[:, None])[:, :, None]
    q_bwd = qc * jnp.exp(log_g[None, :] * (C - idx)[:, None])[:, :, None]
    out = out + jnp.einsum('bnihd,nbhde->bnihe', q_fwd, s_fwd) + jnp.einsum('bnihd,nbhde->bnihe', q_bwd, s_bwd)
    return out.reshape(Bsz, L, H, d)


def conv_ffn(x, w_up, conv_w, conv_b, w_down):
    h = x @ w_up
    val, act = jnp.split(h, 2, axis=-1)
    act = lax.conv_general_dilated(act, conv_w[:, None, :], window_strides=(1,), padding=((1, 1),),
                                   dimension_numbers=('NWC', 'WIO', 'NWC'), feature_group_count=D_FF) + conv_b
    return (jax.nn.silu(act) * val) @ w_down


def encoder_layer(x, w_in, attn_sink, attn_out_g, lam_re, lam_im, log_dt, b_re, b_im, c_re, c_im,
                  ssm_d, glu_w, glu_b, ssm_out_g, w_out, ln1_g, ln1_b, w_up, conv_w, conv_b, w_down,
                  ln2_g, ln2_b):
    Bsz, L, _ = x.shape
    h = x @ w_in
    splits = np.cumsum(IN_SPLITS)[:-1].tolist()
    q, k, v, u, rq, rk, rv, rg = jnp.split(h, splits, axis=-1)
    inv = ROPE_THETA ** (-jnp.arange(0, ROPE_DIM, 2, dtype=F32) / ROPE_DIM)
    q = rotate(q.reshape(Bsz, L, ATTN_HEADS, HEAD_DIM), inv)
    k = rotate(k.reshape(Bsz, L, ATTN_KV_HEADS, HEAD_DIM), inv)
    v = v.reshape(Bsz, L, ATTN_KV_HEADS, HEAD_DIM)
    a = windowed_attention(q, k, v, attn_sink).reshape(Bsz, L, ATTN_WIDTH)
    a = rms_norm(a, attn_out_g).astype(x.dtype)
    s = jax.nn.gelu(s5_bidirectional(u, lam_re, lam_im, log_dt, b_re, b_im, c_re, c_im, ssm_d))
    s = s * jax.nn.sigmoid(s @ glu_w.astype(F32) + glu_b.astype(F32))
    s = rms_norm(s, ssm_out_g).astype(x.dtype)
    r = retention(rq.reshape(Bsz, L, RET_HEADS, HEAD_DIM), rk.reshape(Bsz, L, RET_HEADS, HEAD_DIM),
                  rv.reshape(Bsz, L, RET_HEADS, HEAD_DIM))
    mu = jnp.mean(r, -1, keepdims=True)
    rc = r - mu
    r = rc * lax.rsqrt(jnp.mean(rc * rc, -1, keepdims=True) + EPS)
    r = (r.reshape(Bsz, L, RET_WIDTH) * jax.nn.silu(rg.astype(F32))).astype(x.dtype)
    mix = jnp.concatenate([a, s, r], axis=-1) @ w_out
    x = layer_norm(DN_ALPHA * x + mix, ln1_g, ln1_b)
    x = layer_norm(DN_ALPHA * x + conv_ffn(x, w_up, conv_w, conv_b, w_down), ln2_g, ln2_b)
    return x


def trunk(x, params):
    for l in range(DEPTH):
        x = encoder_layer(x, *[p[l] for p in params])
    return x


def setup_inputs(seed: int = 0) -> dict:
    key = jax.random.key(seed)
    ks = jax.random.split(key, 26)
    nrm = lambda k, shape, scale: jax.random.normal(k, shape, F32) * scale
    G, P, Hg = SSM_GROUPS, SSM_STATE, SSM_GROUP
    lam_im_base = jnp.pi * jnp.arange(P, dtype=F32)
    return {
        "x_prompt": nrm(ks[0], (BATCH, SEQ, D_MODEL), 1.0),
        "x_sample": nrm(ks[1], (DEC_BATCH, DEC_SEQ, D_MODEL), 1.0),
        "w_in": nrm(ks[2], (DEPTH, D_MODEL, IN_WIDTH), D_MODEL ** -0.5),
        "attn_sink": nrm(ks[3], (DEPTH, ATTN_HEADS), 0.5),
        "attn_out_g": 1.0 + nrm(ks[4], (DEPTH, ATTN_WIDTH), 0.02),
        "ssm_lambda_re": -0.5 + nrm(ks[5], (DEPTH, 2, G, P), 0.01),
        "ssm_lambda_im": lam_im_base + nrm(ks[6], (DEPTH, 2, G, P), 0.01),
        "ssm_log_dt": jax.random.uniform(ks[7], (DEPTH, 2, G), F32, math.log(0.001), math.log(0.1)),
        "ssm_b_re": nrm(ks[8], (DEPTH, 2, G, P, Hg), (2 * Hg) ** -0.5),
        "ssm_b_im": nrm(ks[9], (DEPTH, 2, G, P, Hg), (2 * Hg) ** -0.5),
        "ssm_c_re": nrm(ks[10], (DEPTH, 2, G, Hg, P), (2 * P) ** -0.5),
        "ssm_c_im": nrm(ks[11], (DEPTH, 2, G, Hg, P), (2 * P) ** -0.5),
        "ssm_d": nrm(ks[12], (DEPTH, SSM_WIDTH), 1.0),
        "ssm_glu_w": nrm(ks[13], (DEPTH, SSM_WIDTH, SSM_WIDTH), SSM_WIDTH ** -0.5),
        "ssm_glu_b": nrm(ks[14], (DEPTH, SSM_WIDTH), 0.01),
        "ssm_out_g": 1.0 + nrm(ks[15], (DEPTH, SSM_WIDTH), 0.02),
        "w_out": nrm(ks[16], (DEPTH, MIX_WIDTH, D_MODEL), DN_BETA * MIX_WIDTH ** -0.5),
        "ln1_g": 1.0 + nrm(ks[17], (DEPTH, D_MODEL), 0.02),
        "ln1_b": nrm(ks[18], (DEPTH, D_MODEL), 0.02),
        "ffn_w_up": nrm(ks[19], (DEPTH, D_MODEL, 2 * D_FF), D_MODEL ** -0.5),
        "ffn_conv_w": nrm(ks[20], (DEPTH, FFN_CONV, D_FF), FFN_CONV ** -0.5),
        "ffn_conv_b": nrm(ks[21], (DEPTH, D_FF), 0.01),
        "ffn_w_down": nrm(ks[22], (DEPTH, D_FF, D_MODEL), DN_BETA * D_FF ** -0.5),
        "ln2_g": 1.0 + nrm(ks[23], (DEPTH, D_MODEL), 0.02),
        "ln2_b": nrm(ks[24], (DEPTH, D_MODEL), 0.02),
    }


def reference(x_prompt, x_sample, w_in, attn_sink, attn_out_g, ssm_lambda_re, ssm_lambda_im, ssm_log_dt,
              ssm_b_re, ssm_b_im, ssm_c_re, ssm_c_im, ssm_d, ssm_glu_w, ssm_glu_b, ssm_out_g, w_out,
              ln1_g, ln1_b, ffn_w_up, ffn_conv_w, ffn_conv_b, ffn_w_down, ln2_g, ln2_b):
    params = (w_in, attn_sink, attn_out_g, ssm_lambda_re, ssm_lambda_im, ssm_log_dt, ssm_b_re, ssm_b_im,
              ssm_c_re, ssm_c_im, ssm_d, ssm_glu_w, ssm_glu_b, ssm_out_g, w_out, ln1_g, ln1_b,
              ffn_w_up, ffn_conv_w, ffn_conv_b, ffn_w_down, ln2_g, ln2_b)
    y_prompt = trunk(x_prompt, params)
    y_sample = trunk(x_sample, params)
    return (y_prompt, y_sample)
```

```python
import functools
import math

import numpy as np
import jax
import jax.numpy as jnp
from jax import lax
from jax.experimental import pallas as pl
from jax.experimental.pallas import tpu as pltpu

F32 = jnp.float32
BF16 = jnp.bfloat16

D_MODEL = 1024
DEPTH = 4
HEAD_DIM = 64
ATTN_WIDTH = 384
ATTN_HEADS = 6
KV_WIDTH = 128
ATTN_WINDOW = 128
ROPE_THETA = 500000.0
ROPE_DIM = 16
SSM_WIDTH = 256
SSM_GROUP = 16
SSM_GROUPS = 16
SSM_STATE = 64
SSM_LANES = SSM_GROUPS * SSM_STATE
RET_WIDTH = 384
RET_HEADS = 6
RET_CHUNK = 128
RET_THETA = 10000.0
MIX_WIDTH = 1024
IN_WIDTH = 2432
D_FF = 2816
DN_ALPHA = (2 * DEPTH) ** 0.25
EPS = 1e-5

LANES = 128
HALO = 8
VMEM_LIMIT = 56 * 1024 * 1024

OFF_KV = 0
OFF_Q = 256
OFF_U = 640
OFF_RQK = 896
OFF_RV = 1664
OFF_RG = 2048

S5_CHUNK = 128
FF_CHUNK = 256


def _cparams(*sem):
    return pltpu.CompilerParams(dimension_semantics=sem, vmem_limit_bytes=VMEM_LIMIT)


def _const_spec(shape):
    nd = len(shape)
    return pl.BlockSpec(shape, lambda *_: (0,) * nd)


def _dot(a, b):
    return jnp.dot(a, b, preferred_element_type=F32)


def _dot_nt(a, b):
    return lax.dot_general(a, b, (((1,), (1,)), ((), ())), preferred_element_type=F32)


def _dot_tn(a, b):
    return lax.dot_general(a, b, (((0,), (0,)), ((), ())), preferred_element_type=F32)


def _rotate_cols(x, c, s_up, s_dn, shift):
    outs = []
    for j in range(x.shape[1] // LANES):
        xc = x[:, j * LANES:(j + 1) * LANES]
        up = pltpu.roll(xc, LANES - shift, 1)
        dn = pltpu.roll(xc, shift, 1)
        outs.append(xc * c + up * s_up + dn * s_dn)
    return outs[0] if len(outs) == 1 else jnp.concatenate(outs, axis=1)


def _in_proj_kernel(x_ref, w_ref, ak_ref, aq_ref, rq_ref, rk_ref, kvq_ref, u_ref, rqkv_ref, rg_ref):
    xb = x_ref[...].astype(BF16)
    kv = _dot(xb, w_ref[:, OFF_KV:OFF_Q])
    k = _rotate_cols(kv[:, :LANES], ak_ref[0], ak_ref[1], ak_ref[2], ROPE_DIM // 2)
    kvq_ref[:, 0:LANES] = k.astype(BF16)
    kvq_ref[:, LANES:2 * LANES] = kv[:, LANES:].astype(BF16)
    q = _dot(xb, w_ref[:, OFF_Q:OFF_U])
    q = _rotate_cols(q, aq_ref[0], aq_ref[1], aq_ref[2], ROPE_DIM // 2)
    kvq_ref[:, 2 * LANES:] = q.astype(BF16)
    u_ref[...] = _dot(xb, w_ref[:, OFF_U:OFF_RQK])
    rq = _dot(xb, w_ref[:, OFF_RQK:OFF_RQK + RET_WIDTH])
    rq = _rotate_cols(rq, rq_ref[0], rq_ref[1], rq_ref[2], HEAD_DIM // 2)
    rqkv_ref[:, 0:RET_WIDTH] = rq.astype(BF16)
    rk = _dot(xb, w_ref[:, OFF_RQK + RET_WIDTH:OFF_RV])
    rk = _rotate_cols(rk, rk_ref[0], rk_ref[1], rk_ref[2], HEAD_DIM // 2)
    rqkv_ref[:, RET_WIDTH:2 * RET_WIDTH] = rk.astype(BF16)
    rqkv_ref[:, 2 * RET_WIDTH:] = _dot(xb, w_ref[:, OFF_RV:OFF_RG]).astype(BF16)
    rg_ref[...] = _dot(xb, w_ref[:, OFF_RG:])


def _in_proj(x, w, tabs, seq_len, tm):
    n = x.shape[0]
    per_seq = seq_len // tm
    tab_spec = pl.BlockSpec((3, tm, LANES), lambda i: (0, i % per_seq, 0))
    row = lambda width: pl.BlockSpec((tm, width), lambda i: (i, 0))
    return pl.pallas_call(
        _in_proj_kernel,
        grid=(n // tm,),
        in_specs=[row(D_MODEL), _const_spec((D_MODEL, IN_WIDTH)), tab_spec, tab_spec, tab_spec, tab_spec],
        out_specs=[row(640), row(SSM_WIDTH), row(3 * RET_WIDTH), row(RET_WIDTH)],
        out_shape=[jax.ShapeDtypeStruct((n, 640), BF16), jax.ShapeDtypeStruct((n, SSM_WIDTH), F32),
                   jax.ShapeDtypeStruct((n, 3 * RET_WIDTH), BF16), jax.ShapeDtypeStruct((n, RET_WIDTH), F32)],
        compiler_params=_cparams("parallel"),
        name="in_proj",
    )(x, w, tabs["ak"], tabs["aq"], tabs["rq"], tabs["rk"])


def _attn_kernel(sink_ref, main_ref, prev_ref, next_ref, g_ref, o_ref, *, seq_len, tq):
    i = pl.program_id(1)
    nsub = tq // ATTN_WINDOW
    kv_all = jnp.concatenate([prev_ref[...], main_ref[:, 0:2 * LANES], next_ref[...]], axis=0).astype(F32)
    lane = lax.broadcasted_iota(jnp.int32, kv_all.shape, 1) & (LANES - 1)
    low = lane < HEAD_DIM
    swapped = jnp.concatenate([pltpu.roll(kv_all[:, :LANES], HEAD_DIM, 1),
                               pltpu.roll(kv_all[:, LANES:], HEAD_DIM, 1)], axis=1)
    zero = jnp.zeros_like(kv_all)
    lo = (jnp.where(low, kv_all, zero).astype(BF16), jnp.where(low, swapped, zero).astype(BF16))
    hi = (jnp.where(low, zero, swapped).astype(BF16), jnp.where(low, zero, kv_all).astype(BF16))

    qi = lax.broadcasted_iota(jnp.int32, (ATTN_WINDOW, 3 * ATTN_WINDOW), 0)
    kj = lax.broadcasted_iota(jnp.int32, (ATTN_WINDOW, 3 * ATTN_WINDOW), 1) - ATTN_WINDOW
    band = jnp.abs(kj - qi) <= ATTN_WINDOW
    half = lax.broadcasted_iota(jnp.int32, (ATTN_WINDOW, LANES), 1) < HEAD_DIM
    g = g_ref[...]

    for sb in range(nsub):
        r0 = sb * ATTN_WINDOW
        kabs = i * tq + r0 + kj
        mask = band & (kabs >= 0) & (kabs < seq_len)
        cols = []
        for c in range(3):
            qc = main_ref[r0:r0 + ATTN_WINDOW, (2 + c) * LANES:(3 + c) * LANES]
            acc = None
            inv = []
            for e in range(2):
                h = 2 * c + e
                kvh = h // 3
                src = (lo if e == 0 else hi)[kvh]
                keys = src[r0:r0 + 3 * ATTN_WINDOW, 0:LANES]
                vals = src[r0:r0 + 3 * ATTN_WINDOW, LANES:2 * LANES]
                s = jnp.where(mask, _dot_nt(qc, keys), -1e30)
                sk = sink_ref[h]
                m = jnp.maximum(jnp.max(s, axis=-1, keepdims=True), sk)
                p = jnp.exp(s - m)
                den = jnp.sum(p, axis=-1, keepdims=True) + jnp.exp(sk - m)
                inv.append(1.0 / den)
                pv = _dot(p.astype(BF16), vals)
                acc = pv if acc is None else acc + pv
            cols.append(acc * jnp.where(half, inv[0], inv[1]))
        a = jnp.concatenate(cols, axis=1)
        a = a * lax.rsqrt(jnp.mean(a * a, axis=-1, keepdims=True) + EPS) * g
        o_ref[r0:r0 + ATTN_WINDOW, :] = a.astype(BF16)


def _attention(kvq, sink, gain, batch, seq_len, tq):
    n = kvq.shape[0]
    per_seq = seq_len // tq
    sub = tq // ATTN_WINDOW
    last = n // ATTN_WINDOW - 1
    main = pl.BlockSpec((tq, 640), lambda b, i: (b * per_seq + i, 0))
    prev = pl.BlockSpec((ATTN_WINDOW, 2 * LANES), lambda b, i: (jnp.maximum((b * per_seq + i) * sub - 1, 0), 0))
    nxt = pl.BlockSpec((ATTN_WINDOW, 2 * LANES), lambda b, i: (jnp.minimum((b * per_seq + i + 1) * sub, last), 0))
    return pl.pallas_call(
        functools.partial(_attn_kernel, seq_len=seq_len, tq=tq),
        grid=(batch, per_seq),
        in_specs=[pl.BlockSpec(memory_space=pltpu.SMEM), main, prev, nxt, _const_spec((1, ATTN_WIDTH))],
        out_specs=pl.BlockSpec((tq, ATTN_WIDTH), lambda b, i: (b * per_seq + i, 0)),
        out_shape=jax.ShapeDtypeStruct((n, ATTN_WIDTH), BF16),
        compiler_params=_cparams("parallel", "parallel"),
        name="attention",
    )(sink, kvq, kvq, kvq, gain)


def _ret_kernel(fwd_ref, bwd_ref, decay_ref, gq_ref, gk_ref, gc_ref, bd_ref, rf_ref, rb_ref, sf_ref, sb_ref, *, tr):
    j = pl.program_id(1)

    @pl.when(j == 0)
    def _():
        sf_ref[...] = jnp.zeros_like(sf_ref)
        sb_ref[...] = jnp.zeros_like(sb_ref)

    nch = tr // RET_CHUNK
    half = lax.broadcasted_iota(jnp.int32, (RET_CHUNK, LANES), 1) < HEAD_DIM
    zero = jnp.zeros((RET_CHUNK, LANES), F32)
    bd = bd_ref[...]
    for ci in range(nch):
        r0 = ci * RET_CHUNK
        for c in range(3):
            q = fwd_ref[r0:r0 + RET_CHUNK, c * LANES:(c + 1) * LANES]
            k = fwd_ref[r0:r0 + RET_CHUNK, RET_WIDTH + c * LANES:RET_WIDTH + (c + 1) * LANES]
            v = fwd_ref[r0:r0 + RET_CHUNK, 2 * RET_WIDTH + c * LANES:2 * RET_WIDTH + (c + 1) * LANES]
            kf = k.astype(F32)
            vf = v.astype(F32)
            out = None
            for e in range(2):
                sel = half if e == 0 else jnp.logical_not(half)
                ke = jnp.where(sel, kf, zero).astype(BF16)
                ve = jnp.where(sel, vf, zero).astype(BF16)
                s = _dot_nt(q, ke) * decay_ref[2 * c + e]
                o = _dot(s.astype(BF16), ve)
                out = o if out is None else out + o
            qf = (q.astype(F32) * gq_ref[0, :, c * LANES:(c + 1) * LANES]).astype(BF16)
            out = out + _dot(qf, sf_ref[c].astype(BF16))
            rf_ref[r0:r0 + RET_CHUNK, c * LANES:(c + 1) * LANES] = out
            kd = (kf * gk_ref[0, :, c * LANES:(c + 1) * LANES]).astype(BF16)
            sf_ref[c] = gc_ref[c] * sf_ref[c] + _dot_tn(kd, v) * bd
        r0 = (nch - 1 - ci) * RET_CHUNK
        for c in range(3):
            q = bwd_ref[r0:r0 + RET_CHUNK, c * LANES:(c + 1) * LANES]
            k = bwd_ref[r0:r0 + RET_CHUNK, RET_WIDTH + c * LANES:RET_WIDTH + (c + 1) * LANES]
            v = bwd_ref[r0:r0 + RET_CHUNK, 2 * RET_WIDTH + c * LANES:2 * RET_WIDTH + (c + 1) * LANES]
            qb = (q.astype(F32) * gq_ref[1, :, c * LANES:(c + 1) * LANES]).astype(BF16)
            rb_ref[r0:r0 + RET_CHUNK, c * LANES:(c + 1) * LANES] = _dot(qb, sb_ref[c].astype(BF16))
            kd = (k.astype(F32) * gk_ref[1, :, c * LANES:(c + 1) * LANES]).astype(BF16)
            sb_ref[c] = gc_ref[c] * sb_ref[c] + _dot_tn(kd, v) * bd


def _retention(rqkv, consts, batch, seq_len, tr):
    n = rqkv.shape[0]
    per_seq = seq_len // tr
    fwd_map = lambda b, j: (b * per_seq + j, 0)
    bwd_map = lambda b, j: (b * per_seq + per_seq - 1 - j, 0)
    return pl.pallas_call(
        functools.partial(_ret_kernel, tr=tr),
        grid=(batch, per_seq),
        in_specs=[pl.BlockSpec((tr, 3 * RET_WIDTH), fwd_map), pl.BlockSpec((tr, 3 * RET_WIDTH), bwd_map),
                  _const_spec((RET_HEADS, RET_CHUNK, RET_CHUNK)), _const_spec((2, RET_CHUNK, RET_WIDTH)),
                  _const_spec((2, RET_CHUNK, RET_WIDTH)), _const_spec((3, LANES, LANES)),
                  _const_spec((LANES, LANES))],
        out_specs=[pl.BlockSpec((tr, RET_WIDTH), fwd_map), pl.BlockSpec((tr, RET_WIDTH), bwd_map)],
        out_shape=[jax.ShapeDtypeStruct((n, RET_WIDTH), F32)] * 2,
        scratch_shapes=[pltpu.VMEM((3, LANES, LANES), F32), pltpu.VMEM((3, LANES, LANES), F32)],
        compiler_params=_cparams("parallel", "arbitrary"),
        name="retention",
    )(rqkv, rqkv, consts["decay"], consts["gq"], consts["gk"], consts["gc"], consts["bd"])


def _cmul(ar, ai, br, bi):
    return ar * br - ai * bi, ar * bi + ai * br


def _s5_kernel(uf_ref, ub_ref, bmat_ref, cmat_ref, linv_ref, lpow_ref, lam_ref, tri_ref, d_ref,
               yf_ref, yb_ref, st_ref, *, ts):
    j = pl.program_id(1)

    @pl.when(j == 0)
    def _():
        st_ref[...] = jnp.zeros_like(st_ref)

    nch = ts // S5_CHUNK
    n = SSM_LANES
    for z, (u_ref, y_ref) in enumerate(((uf_ref, yf_ref), (ub_ref, yb_ref))):
        edge = S5_CHUNK - 1 if z == 0 else 0
        for ci in range(nch):
            r0 = (ci if z == 0 else nch - 1 - ci) * S5_CHUNK
            u = u_ref[r0:r0 + S5_CHUNK, :]
            bu = _dot(u.astype(BF16), bmat_ref[z])
            zr, zi = _cmul(linv_ref[z, :, :n], linv_ref[z, :, n:], bu[:, :n], bu[:, n:])
            w = _dot(tri_ref[z], jnp.concatenate([zr, zi], axis=1).astype(BF16))
            cr, ci_ = _cmul(lam_ref[z, :, :n], lam_ref[z, :, n:], st_ref[z, 0:1, :n], st_ref[z, 0:1, n:])
            xr, xi = _cmul(lpow_ref[z, :, :n], lpow_ref[z, :, n:], w[:, :n] + cr, w[:, n:] + ci_)
            st_ref[z, 0:1, :n] = xr[edge:edge + 1, :]
            st_ref[z, 0:1, n:] = xi[edge:edge + 1, :]
            y = _dot(jnp.concatenate([xr, xi], axis=1).astype(BF16), cmat_ref[z])
            if z == 0:
                y = y + d_ref[...] * u
            y_ref[r0:r0 + S5_CHUNK, :] = y


def _s5(u, p, batch, seq_len, ts):
    n = u.shape[0]
    per_seq = seq_len // ts
    fwd_map = lambda b, j: (b * per_seq + j, 0)
    bwd_map = lambda b, j: (b * per_seq + per_seq - 1 - j, 0)
    blk = lambda m: pl.BlockSpec((ts, SSM_WIDTH), m)
    return pl.pallas_call(
        functools.partial(_s5_kernel, ts=ts),
        grid=(batch, per_seq),
        in_specs=[blk(fwd_map), blk(bwd_map),
                  _const_spec((2, SSM_WIDTH, 2 * SSM_LANES)), _const_spec((2, 2 * SSM_LANES, SSM_WIDTH)),
                  _const_spec((2, S5_CHUNK, 2 * SSM_LANES)), _const_spec((2, S5_CHUNK, 2 * SSM_LANES)),
                  _const_spec((2, 1, 2 * SSM_LANES)), _const_spec((2, S5_CHUNK, S5_CHUNK)),
                  _const_spec((1, SSM_WIDTH))],
        out_specs=[blk(fwd_map), blk(bwd_map)],
        out_shape=[jax.ShapeDtypeStruct((n, SSM_WIDTH), F32)] * 2,
        scratch_shapes=[pltpu.VMEM((2, 8, 2 * SSM_LANES), F32)],
        compiler_params=_cparams("parallel", "arbitrary"),
        name="s5",
    )(u, u, p["bmat"], p["cmat"], p["linv"], p["lpow"], p["lam"], p["tri"], p["d"])


def _layer_norm(x, g, b):
    mu = jnp.mean(x, axis=-1, keepdims=True)
    xc = x - mu
    var = jnp.mean(xc * xc, axis=-1, keepdims=True)
    return xc * lax.rsqrt(var + EPS) * g + b


def _split_dot(x, m):
    hi = x.astype(BF16)
    lo = (x - hi.astype(F32)).astype(BF16)
    return _dot(hi, m) + _dot(lo, m)


def _mix_kernel(x_ref, a_ref, yf_ref, yb_ref, rf_ref, rb_ref, rg_ref, gluw_ref, glub_ref, sg_ref, avg_ref,
                wout_ref, g_ref, b_ref, o_ref):
    y = yf_ref[...] + yb_ref[...]
    s = 0.5 * y * (1.0 + jnp.tanh(math.sqrt(2.0 / math.pi) * (y + 0.044715 * (y * y * y))))
    gate = _dot(s.astype(BF16), gluw_ref[...]) + glub_ref[...]
    s = s * (1.0 / (1.0 + jnp.exp(-gate)))
    s = s * lax.rsqrt(jnp.mean(s * s, axis=-1, keepdims=True) + EPS) * sg_ref[...]
    r = rf_ref[...] + rb_ref[...]
    rc = r - _split_dot(r, avg_ref[...])
    r = rc * lax.rsqrt(_split_dot(rc * rc, avg_ref[...]) + EPS)
    rg = rg_ref[...]
    r = r * (rg * (1.0 / (1.0 + jnp.exp(-rg))))
    mix = (_dot(a_ref[...], wout_ref[0:ATTN_WIDTH, :])
           + _dot(s.astype(BF16), wout_ref[ATTN_WIDTH:ATTN_WIDTH + SSM_WIDTH, :])
           + _dot(r.astype(BF16), wout_ref[ATTN_WIDTH + SSM_WIDTH:, :]))
    o_ref[...] = _layer_norm(DN_ALPHA * x_ref[...] + mix, g_ref[...], b_ref[...])


def _mix(x, a, yf, yb, rf, rb, rg, lp, avg, tm):
    n = x.shape[0]
    row = lambda width: pl.BlockSpec((tm, width), lambda i: (i, 0))
    return pl.pallas_call(
        _mix_kernel,
        grid=(n // tm,),
        in_specs=[row(D_MODEL), row(ATTN_WIDTH), row(SSM_WIDTH), row(SSM_WIDTH), row(RET_WIDTH), row(RET_WIDTH),
                  row(RET_WIDTH), _const_spec((SSM_WIDTH, SSM_WIDTH)), _const_spec((1, SSM_WIDTH)),
                  _const_spec((1, SSM_WIDTH)), _const_spec((RET_WIDTH, RET_WIDTH)),
                  _const_spec((MIX_WIDTH, D_MODEL)), _const_spec((1, D_MODEL)), _const_spec((1, D_MODEL))],
        out_specs=row(D_MODEL),
        out_shape=jax.ShapeDtypeStruct((n, D_MODEL), F32),
        compiler_params=_cparams("parallel"),
        name="mix",
    )(x, a, yf, yb, rf, rb, rg, lp["glu_w"], lp["glu_b"], lp["ssm_g"], avg, lp["w_out"], lp["ln1_g"], lp["ln1_b"])


def _ffn_kernel(x_ref, prev_ref, next_ref, wup_ref, cw_ref, cb_ref, wdn_ref, g_ref, b_ref, o_ref, gated_ref,
                *, seq_len, tm):
    i = pl.program_id(0)
    per_seq = seq_len // tm
    first = (i % per_seq) == 0
    last = (i % per_seq) == per_seq - 1
    x = x_ref[...]
    xb = x.astype(BF16)
    xp = jnp.where(first, 0.0, prev_ref[...]).astype(BF16)
    xn = jnp.where(last, 0.0, next_ref[...]).astype(BF16)
    rows = lax.broadcasted_iota(jnp.int32, (tm, FF_CHUNK), 0)
    for c in range(D_FF // FF_CHUNK):
        c0 = c * FF_CHUNK
        w_val = wup_ref[:, c0:c0 + FF_CHUNK]
        w_act = wup_ref[:, D_FF + c0:D_FF + c0 + FF_CHUNK]
        val = _dot(xb, w_val)
        act = _dot(xb, w_act)
        act_p = _dot(xp, w_act)[HALO - 1:HALO, :]
        act_n = _dot(xn, w_act)[0:1, :]
        before = jnp.where(rows == 0, act_p, pltpu.roll(act, 1, 0))
        after = jnp.where(rows == tm - 1, act_n, pltpu.roll(act, tm - 1, 0))
        cv = (before * cw_ref[0:1, c0:c0 + FF_CHUNK] + act * cw_ref[1:2, c0:c0 + FF_CHUNK]
              + after * cw_ref[2:3, c0:c0 + FF_CHUNK] + cb_ref[:, c0:c0 + FF_CHUNK])
        gated_ref[:, c0:c0 + FF_CHUNK] = (cv * (1.0 / (1.0 + jnp.exp(-cv))) * val).astype(BF16)
    ffn = _dot(gated_ref[...], wdn_ref[...])
    o_ref[...] = _layer_norm(DN_ALPHA * x + ffn, g_ref[...], b_ref[...])


def _ffn(x, lp, seq_len, tm):
    n = x.shape[0]
    sub = tm // HALO
    last = n // HALO - 1
    return pl.pallas_call(
        functools.partial(_ffn_kernel, seq_len=seq_len, tm=tm),
        grid=(n // tm,),
        in_specs=[pl.BlockSpec((tm, D_MODEL), lambda i: (i, 0)),
                  pl.BlockSpec((HALO, D_MODEL), lambda i: (jnp.maximum(i * sub - 1, 0), 0)),
                  pl.BlockSpec((HALO, D_MODEL), lambda i: (jnp.minimum((i + 1) * sub, last), 0)),
                  _const_spec((D_MODEL, 2 * D_FF)), _const_spec((3, D_FF)), _const_spec((1, D_FF)),
                  _const_spec((D_FF, D_MODEL)), _const_spec((1, D_MODEL)), _const_spec((1, D_MODEL))],
        out_specs=pl.BlockSpec((tm, D_MODEL), lambda i: (i, 0)),
        out_shape=jax.ShapeDtypeStruct((n, D_MODEL), F32),
        scratch_shapes=[pltpu.VMEM((tm, D_FF), BF16)],
        compiler_params=_cparams("parallel"),
        name="ffn",
    )(x, x, x, lp["w_up"], lp["conv_w"], lp["conv_b"], lp["w_down"], lp["ln2_g"], lp["ln2_b"])


def _rotary_tables(seq_len):
    pos = jnp.arange(seq_len, dtype=F32)[:, None]

    def build(inv_freq, width, scale):
        half = inv_freq.shape[0]
        ang = pos * inv_freq[None, :]
        cos, sin = jnp.cos(ang), jnp.sin(ang)
        pad = HEAD_DIM - 2 * half
        c = jnp.concatenate([cos, cos, jnp.ones((seq_len, pad), F32)], axis=1)
        s_up = jnp.concatenate([-sin, jnp.zeros((seq_len, half + pad), F32)], axis=1)
        s_dn = jnp.concatenate([jnp.zeros((seq_len, half), F32), sin, jnp.zeros((seq_len, pad), F32)], axis=1)
        t = jnp.stack([c, s_up, s_dn]) * scale
        return jnp.tile(t, (1, 1, LANES // HEAD_DIM))

    inv_a = ROPE_THETA ** (-jnp.arange(0, ROPE_DIM, 2, dtype=F32) / ROPE_DIM)
    inv_r = 1.0 / (RET_THETA ** jnp.linspace(0.0, 1.0, HEAD_DIM // 2, dtype=F32))
    scale = HEAD_DIM ** -0.5
    return {"ak": build(inv_a, HEAD_DIM, 1.0), "aq": build(inv_a, HEAD_DIM, scale),
            "rq": build(inv_r, HEAD_DIM, 1.0), "rk": build(inv_r, HEAD_DIM, scale)}


def _retention_consts():
    c = RET_CHUNK
    log_g = jnp.log(1.0 - 2.0 ** (-5.0 - jnp.arange(RET_HEADS, dtype=F32)))
    idx = jnp.arange(c, dtype=F32)
    decay = jnp.exp(log_g[:, None, None] * jnp.abs(idx[:, None] - idx[None, :]))
    per_lane = lambda e: jnp.repeat(jnp.exp(log_g[None, :] * e[:, None]), HEAD_DIM, axis=1)
    gq = jnp.stack([per_lane(idx + 1.0), per_lane(c - idx)])
    gk = jnp.stack([per_lane(c - 1.0 - idx), per_lane(idx)])
    g_chunk = jnp.repeat(jnp.exp(log_g * c), HEAD_DIM).reshape(3, LANES)
    gc = jnp.broadcast_to(g_chunk[:, :, None], (3, LANES, LANES))
    blk = jnp.arange(LANES) // HEAD_DIM
    bd = (blk[:, None] == blk[None, :]).astype(F32)
    seg = jnp.arange(RET_WIDTH) // HEAD_DIM
    avg = ((seg[:, None] == seg[None, :]).astype(F32) / HEAD_DIM).astype(BF16)
    return {"decay": decay, "gq": gq, "gk": gk, "gc": gc, "bd": bd}, avg


def _powers(br, bi, count):
    pr, pi = jnp.ones_like(br)[None], jnp.zeros_like(bi)[None]
    sr, si = br, bi
    while pr.shape[0] < count:
        nr, ni = _cmul(pr, pi, sr[None], si[None])
        pr, pi = jnp.concatenate([pr, nr]), jnp.concatenate([pi, ni])
        sr, si = _cmul(sr, si, sr, si)
    return pr[:count], pi[:count]


def _s5_params(lam_re, lam_im, log_dt, b_re, b_im, c_re, c_im, d_skip):
    t = S5_CHUNK
    dt = jnp.exp(log_dt)[:, :, None]
    mag = jnp.exp(lam_re * dt)
    lbr, lbi = mag * jnp.cos(lam_im * dt), mag * jnp.sin(lam_im * dt)
    den = lam_re * lam_re + lam_im * lam_im
    cfr, cfi = _cmul(lbr - 1.0, lbi, lam_re / den, -lam_im / den)
    bbr, bbi = _cmul(cfr[..., None], cfi[..., None], b_re, b_im)
    eye = jnp.eye(SSM_GROUPS, dtype=F32)

    def b_block(m):
        return jnp.einsum('zgph,gk->zghkp', m, eye).reshape(2, SSM_WIDTH, SSM_LANES)

    def c_block(m):
        return jnp.einsum('zghp,gk->zgpkh', m, eye).reshape(2, SSM_LANES, SSM_WIDTH)

    bmat = jnp.concatenate([b_block(bbr), b_block(bbi)], axis=2).astype(BF16)
    cmat = jnp.concatenate([c_block(c_re), -c_block(c_im)], axis=1).astype(BF16)
    flat = lambda a: a.reshape(2, SSM_LANES)
    pr, pi = _powers(flat(lbr), flat(lbi), t)
    mag2 = lbr * lbr + lbi * lbi
    qr, qi = _powers(flat(lbr / mag2), flat(-lbi / mag2), t)
    pw = jnp.concatenate([pr, pi], axis=2)
    iv = jnp.concatenate([qr, qi], axis=2)
    lpow = jnp.stack([pw[:, 0], pw[::-1, 1]])
    linv = jnp.stack([iv[:, 0], iv[::-1, 1]])
    lam = jnp.concatenate([flat(lbr), flat(lbi)], axis=1)[:, None, :]
    tri = jnp.tril(jnp.ones((t, t), F32))
    tri = jnp.stack([tri, tri.T]).astype(BF16)
    return {"bmat": bmat, "cmat": cmat, "linv": linv, "lpow": lpow, "lam": lam, "tri": tri,
            "d": d_skip.reshape(1, SSM_WIDTH)}


def _tile(seq_len, want):
    return min(want, seq_len)


def _trunk(x3, layers, ret_consts, avg):
    batch, seq_len, _ = x3.shape
    x = x3.reshape(batch * seq_len, D_MODEL)
    tabs = _rotary_tables(seq_len)
    for lp in layers:
        kvq, u, rqkv, rg = _in_proj(x, lp["w_in"], tabs, seq_len, _tile(seq_len, 512))
        a = _attention(kvq, lp["sink"], lp["attn_g"], batch, seq_len, _tile(seq_len, 512))
        rf, rb = _retention(rqkv, ret_consts, batch, seq_len, _tile(seq_len, 512))
        yf, yb = _s5(u, lp["s5"], batch, seq_len, _tile(seq_len, 512))
        x = _mix(x, a, yf, yb, rf, rb, rg, lp, avg, _tile(seq_len, 512))
        x = _ffn(x, lp, seq_len, _tile(seq_len, 256))
    return x.reshape(batch, seq_len, D_MODEL)


def _prepare_layers(w_in, attn_sink, attn_out_g, lam_re, lam_im, log_dt, b_re, b_im, c_re, c_im, ssm_d, glu_w,
                    glu_b, ssm_out_g, w_out, ln1_g, ln1_b, w_up, conv_w, conv_b, w_down, ln2_g, ln2_b):
    depth = w_in.shape[0]
    w_in = jnp.concatenate([w_in[:, :, 384:640], w_in[:, :, 0:384], w_in[:, :, 640:]], axis=2).astype(BF16)
    layers = []
    for l in range(depth):
        layers.append({
            "w_in": w_in[l], "sink": attn_sink[l], "attn_g": attn_out_g[l].reshape(1, ATTN_WIDTH),
            "s5": _s5_params(lam_re[l], lam_im[l], log_dt[l], b_re[l], b_im[l], c_re[l], c_im[l], ssm_d[l]),
            "glu_w": glu_w[l].astype(BF16), "glu_b": glu_b[l].reshape(1, SSM_WIDTH),
            "ssm_g": ssm_out_g[l].reshape(1, SSM_WIDTH), "w_out": w_out[l].astype(BF16),
            "ln1_g": ln1_g[l].reshape(1, D_MODEL), "ln1_b": ln1_b[l].reshape(1, D_MODEL),
            "w_up": w_up[l].astype(BF16), "conv_w": conv_w[l], "conv_b": conv_b[l].reshape(1, D_FF),
            "w_down": w_down[l].astype(BF16),
            "ln2_g": ln2_g[l].reshape(1, D_MODEL), "ln2_b": ln2_b[l].reshape(1, D_MODEL),
        })
    return layers


def kernel(x_prompt, x_sample, w_in, attn_sink, attn_out_g, ssm_lambda_re, ssm_lambda_im, ssm_log_dt, ssm_b_re, ssm_b_im, ssm_c_re, ssm_c_im, ssm_d, ssm_glu_w, ssm_glu_b, ssm_out_g, w_out, ln1_g, ln1_b, ffn_w_up, ffn_conv_w, ffn_conv_b, ffn_w_down, ln2_g, ln2_b):
    layers = _prepare_layers(w_in, attn_sink, attn_out_g, ssm_lambda_re, ssm_lambda_im, ssm_log_dt, ssm_b_re,
                             ssm_b_im, ssm_c_re, ssm_c_im, ssm_d, ssm_glu_w, ssm_glu_b, ssm_out_g, w_out, ln1_g,
                             ln1_b, ffn_w_up, ffn_conv_w, ffn_conv_b, ffn_w_down, ln2_g, ln2_b)
    ret_consts, avg = _retention_consts()
    return (_trunk(x_prompt, layers, ret_consts, avg), _trunk(x_sample, layers, ret_consts, avg))
```

```python
import functools
import math

import jax
import jax.numpy as jnp
from jax import lax
from jax.experimental import pallas as pl
from jax.experimental.pallas import tpu as pltpu

F32 = jnp.float32
BF16 = jnp.bfloat16

D_MODEL = 1024
DEPTH = 4
HEAD_DIM = 64
ATTN_WIDTH = 384
ATTN_HEADS = 6
ATTN_KV_HEADS = 2
ATTN_WINDOW = 128
ROPE_THETA = 500000.0
ROPE_DIM = 16
SSM_WIDTH = 256
SSM_GROUP = 16
SSM_GROUPS = 16
SSM_STATE = 64
SSM_LANES = SSM_GROUPS * SSM_STATE
RET_WIDTH = 384
RET_HEADS = 6
RET_CHUNK = 128
RET_THETA = 10000.0
MIX_WIDTH = 1024
IN_WIDTH = 2432
D_FF = 2816
DN_ALPHA = (2 * DEPTH) ** 0.25
EPS = 1e-5

LANES = 128
MXU_DIM = 256
BF16_ROWS = 16
VMEM_LIMIT = 56 * 1024 * 1024

KVQ_WIDTH = 2 * LANES + ATTN_WIDTH
S5_CHUNK = 128
S5_SLAB = 2 * LANES
S5_SLABS = 2 * SSM_LANES // S5_SLAB
FF_CHUNK = 256
FFN_ROW_SPLIT = 2

TILE = {"in_proj": 512, "attention": 512, "retention": 512, "s5": 512, "mix": 512, "ffn": 512}

IN_COLUMNS = ([("k", 0), ("v", 0)] + [("q", c) for c in range(3)] + [("u", c) for c in range(2)]
              + [("rq", c) for c in range(3)] + [("rk", c) for c in range(3)] + [("rv", c) for c in range(3)]
              + [("rg", c) for c in range(3)])


def _cparams(*sem):
    return pltpu.CompilerParams(dimension_semantics=sem, vmem_limit_bytes=VMEM_LIMIT)


def _const_spec(shape):
    nd = len(shape)
    return pl.BlockSpec(shape, lambda *_: (0,) * nd)


def _dot(a, b):
    return jnp.dot(a, b, preferred_element_type=F32)


def _dot_nt(a, b):
    return lax.dot_general(a, b, (((1,), (1,)), ((), ())), preferred_element_type=F32)


def _dot_tn(a, b):
    return lax.dot_general(a, b, (((0,), (0,)), ((), ())), preferred_element_type=F32)


def _rotate(xc, tab_ref, shift):
    up = pltpu.roll(xc, LANES - shift, 1)
    dn = pltpu.roll(xc, shift, 1)
    return xc * tab_ref[0] + up * tab_ref[1] + dn * tab_ref[2]


def _in_proj_kernel(x_ref, w_ref, ak_ref, aq_ref, rq_ref, rk_ref, kvq_ref, u_ref, rqkv_ref, rg_ref):
    xb = x_ref[...].astype(BF16)
    roles = {
        "k": (kvq_ref, 0, ak_ref, ROPE_DIM // 2), "v": (kvq_ref, LANES, None, 0),
        "q": (kvq_ref, 2 * LANES, aq_ref, ROPE_DIM // 2), "u": (u_ref, 0, None, 0),
        "rq": (rqkv_ref, 0, rq_ref, HEAD_DIM // 2), "rk": (rqkv_ref, RET_WIDTH, rk_ref, HEAD_DIM // 2),
        "rv": (rqkv_ref, 2 * RET_WIDTH, None, 0), "rg": (rg_ref, 0, None, 0),
    }
    for c0 in range(0, IN_WIDTH, MXU_DIM):
        width = min(MXU_DIM, IN_WIDTH - c0)
        h = _dot(xb, w_ref[:, c0:c0 + width])
        for j in range(width // LANES):
            role, col = IN_COLUMNS[c0 // LANES + j]
            dst, base, tab, shift = roles[role]
            hc = h[:, j * LANES:(j + 1) * LANES]
            if tab is not None:
                hc = _rotate(hc, tab, shift)
            dst[:, base + col * LANES:base + (col + 1) * LANES] = hc.astype(BF16)


def _in_proj(x, w, tabs, seq_len):
    n = x.shape[0]
    tm = min(TILE["in_proj"], seq_len)
    per_seq = seq_len // tm
    tab_spec = pl.BlockSpec((3, tm, LANES), lambda i: (0, i % per_seq, 0))
    row = lambda width: pl.BlockSpec((tm, width), lambda i: (i, 0))
    widths = (KVQ_WIDTH, SSM_WIDTH, 3 * RET_WIDTH, RET_WIDTH)
    return pl.pallas_call(
        _in_proj_kernel,
        grid=(n // tm,),
        in_specs=[row(D_MODEL), _const_spec((D_MODEL, IN_WIDTH)), tab_spec, tab_spec, tab_spec, tab_spec],
        out_specs=[row(wd) for wd in widths],
        out_shape=[jax.ShapeDtypeStruct((n, wd), BF16) for wd in widths],
        compiler_params=_cparams("parallel"),
        name="in_proj",
    )(x, w, tabs["ak"], tabs["aq"], tabs["rq"], tabs["rk"])


def _attn_kernel(sink_ref, main_ref, prev_ref, next_ref, g_ref, o_ref, *, seq_len, tq):
    i = pl.program_id(1)
    w = ATTN_WINDOW
    kv_all = jnp.concatenate([prev_ref[...], main_ref[:, 0:2 * LANES], next_ref[...]], axis=0).astype(F32)
    low = (lax.broadcasted_iota(jnp.int32, kv_all.shape, 1) & (LANES - 1)) < HEAD_DIM
    zero = jnp.zeros_like(kv_all)
    per_kv = (jnp.where(low, kv_all, zero).astype(BF16), jnp.where(low, zero, kv_all).astype(BF16))

    qi = lax.broadcasted_iota(jnp.int32, (w, w), 0)
    kj = lax.broadcasted_iota(jnp.int32, (w, w), 1)
    g = g_ref[...]

    for sb in range(tq // w):
        r0 = sb * w
        pos0 = i * tq + r0
        left_ok = kj >= qi + jnp.where(pos0 >= w, 0, w)
        right_ok = kj <= qi - jnp.where(pos0 + w < seq_len, 0, w)
        q3 = jnp.concatenate([main_ref[r0:r0 + w, (2 + c) * LANES:(3 + c) * LANES] for c in range(3)], axis=0)
        acc = None
        for kvh in range(ATTN_KV_HEADS):
            keys = per_kv[kvh][r0:r0 + 3 * w, 0:LANES]
            vals = per_kv[kvh][r0:r0 + 3 * w, LANES:2 * LANES]
            s = _dot_nt(q3, keys)
            probs, invs = [], []
            for c in range(3):
                sk = sink_ref[kvh * 3 + c]
                sc = s[c * w:(c + 1) * w, :]
                s_l = jnp.where(left_ok, sc[:, 0:w], -1e30)
                s_m = sc[:, w:2 * w]
                s_r = jnp.where(right_ok, sc[:, 2 * w:], -1e30)
                m = jnp.maximum(jnp.max(jnp.maximum(jnp.maximum(s_l, s_m), s_r), axis=-1, keepdims=True), sk)
                p_l, p_m, p_r = jnp.exp(s_l - m), jnp.exp(s_m - m), jnp.exp(s_r - m)
                den = jnp.sum(p_l + p_m + p_r, axis=-1, keepdims=True) + jnp.exp(sk - m)
                invs.append(1.0 / den)
                probs.append(jnp.concatenate([p_l, p_m, p_r], axis=1).astype(BF16))
            pv = _dot(jnp.concatenate(probs, axis=0), vals) * jnp.concatenate(invs, axis=0)
            acc = pv if acc is None else acc + pv
        a = jnp.concatenate([acc[0:w], acc[w:2 * w], acc[2 * w:]], axis=1)
        a = a * lax.rsqrt(jnp.mean(a * a, axis=-1, keepdims=True) + EPS) * g
        o_ref[r0:r0 + w, :] = a.astype(BF16)


def _attention(kvq, sink, gain, batch, seq_len):
    n = kvq.shape[0]
    tq = min(TILE["attention"], seq_len)
    per_seq = seq_len // tq
    sub = tq // ATTN_WINDOW
    last = n // ATTN_WINDOW - 1
    main = pl.BlockSpec((tq, KVQ_WIDTH), lambda b, i: (b * per_seq + i, 0))
    prev = pl.BlockSpec((ATTN_WINDOW, 2 * LANES), lambda b, i: (jnp.maximum((b * per_seq + i) * sub - 1, 0), 0))
    nxt = pl.BlockSpec((ATTN_WINDOW, 2 * LANES), lambda b, i: (jnp.minimum((b * per_seq + i + 1) * sub, last), 0))
    return pl.pallas_call(
        functools.partial(_attn_kernel, seq_len=seq_len, tq=tq),
        grid=(batch, per_seq),
        in_specs=[pl.BlockSpec(memory_space=pltpu.SMEM), main, prev, nxt, _const_spec((1, ATTN_WIDTH))],
        out_specs=pl.BlockSpec((tq, ATTN_WIDTH), lambda b, i: (b * per_seq + i, 0)),
        out_shape=jax.ShapeDtypeStruct((n, ATTN_WIDTH), BF16),
        compiler_params=_cparams("parallel", "parallel"),
        name="attention",
    )(sink, kvq, kvq, kvq, gain)


def _ret_kernel(fwd_ref, bwd_ref, decay_ref, gq_ref, gk_ref, gc_ref, bd_ref, rf_ref, rb_ref, sf_ref, sb_ref, *, tr):
    j = pl.program_id(1)

    @pl.when(j == 0)
    def _():
        sf_ref[...] = jnp.zeros_like(sf_ref)
        sb_ref[...] = jnp.zeros_like(sb_ref)

    nch = tr // RET_CHUNK
    half = lax.broadcasted_iota(jnp.int32, (RET_CHUNK, LANES), 1) < HEAD_DIM
    zero = jnp.zeros((RET_CHUNK, LANES), F32)
    bd = bd_ref[...]

    def parts(ref, r0, c):
        col = lambda base: ref[r0:r0 + RET_CHUNK, base + c * LANES:base + (c + 1) * LANES]
        return col(0), col(RET_WIDTH), col(2 * RET_WIDTH)

    for ci in range(nch):
        r0 = ci * RET_CHUNK
        for c in range(3):
            lanes = slice(c * LANES, (c + 1) * LANES)
            q, k, v = parts(fwd_ref, r0, c)
            kf, vf = k.astype(F32), v.astype(F32)
            k2 = jnp.concatenate([jnp.where(half, kf, zero), jnp.where(half, zero, kf)], axis=0).astype(BF16)
            v2 = jnp.concatenate([jnp.where(half, vf, zero), jnp.where(half, zero, vf)], axis=0).astype(BF16)
            s = _dot_nt(q, k2) * decay_ref[c]
            qf = (q.astype(F32) * gq_ref[0, :, lanes]).astype(BF16)
            lhs = jnp.concatenate([s.astype(BF16), qf], axis=1)
            rhs = jnp.concatenate([v2, sf_ref[c].astype(BF16)], axis=0)
            rf_ref[r0:r0 + RET_CHUNK, lanes] = _dot(lhs, rhs).astype(BF16)
            kd = (kf * gk_ref[0, :, lanes]).astype(BF16)
            sf_ref[c] = gc_ref[c] * sf_ref[c] + _dot_tn(kd, v) * bd
        r0 = (nch - 1 - ci) * RET_CHUNK
        for c in range(3):
            lanes = slice(c * LANES, (c + 1) * LANES)
            q, k, v = parts(bwd_ref, r0, c)
            qb = (q.astype(F32) * gq_ref[1, :, lanes]).astype(BF16)
            rb_ref[r0:r0 + RET_CHUNK, lanes] = _dot(qb, sb_ref[c].astype(BF16)).astype(BF16)
            kd = (k.astype(F32) * gk_ref[1, :, lanes]).astype(BF16)
            sb_ref[c] = gc_ref[c] * sb_ref[c] + _dot_tn(kd, v) * bd


def _retention(rqkv, consts, batch, seq_len):
    n = rqkv.shape[0]
    tr = min(TILE["retention"], seq_len)
    per_seq = seq_len // tr
    fwd_map = lambda b, j: (b * per_seq + j, 0)
    bwd_map = lambda b, j: (b * per_seq + per_seq - 1 - j, 0)
    return pl.pallas_call(
        functools.partial(_ret_kernel, tr=tr),
        grid=(batch, per_seq),
        in_specs=[pl.BlockSpec((tr, 3 * RET_WIDTH), fwd_map), pl.BlockSpec((tr, 3 * RET_WIDTH), bwd_map),
                  _const_spec((3, RET_CHUNK, 2 * RET_CHUNK)), _const_spec((2, RET_CHUNK, RET_WIDTH)),
                  _const_spec((2, RET_CHUNK, RET_WIDTH)), _const_spec((3, LANES, LANES)),
                  _const_spec((LANES, LANES))],
        out_specs=[pl.BlockSpec((tr, RET_WIDTH), fwd_map), pl.BlockSpec((tr, RET_WIDTH), bwd_map)],
        out_shape=[jax.ShapeDtypeStruct((n, RET_WIDTH), BF16)] * 2,
        scratch_shapes=[pltpu.VMEM((3, LANES, LANES), F32), pltpu.VMEM((3, LANES, LANES), F32)],
        compiler_params=_cparams("parallel", "arbitrary"),
        name="retention",
    )(rqkv, rqkv, consts["decay"], consts["gq"], consts["gk"], consts["gc"], consts["bd"])


def _cmul(ar, ai, br, bi):
    return ar * br - ai * bi, ar * bi + ai * br


def _s5_kernel(uf_ref, ub_ref, bmat_ref, cmat_ref, linv_ref, lpow_ref, lam_ref, tri_ref, d_ref,
               yf_ref, yb_ref, st_ref, x_ref, *, ts):
    j = pl.program_id(1)

    @pl.when(j == 0)
    def _():
        st_ref[...] = jnp.zeros_like(st_ref)

    nch = ts // S5_CHUNK
    for z, (u_ref, y_ref) in enumerate(((uf_ref, yf_ref), (ub_ref, yb_ref))):
        edge = S5_CHUNK - 1 if z == 0 else 0
        u = u_ref[...]
        for s in range(S5_SLABS):
            re = slice(s * S5_SLAB, s * S5_SLAB + LANES)
            im = slice(s * S5_SLAB + LANES, (s + 1) * S5_SLAB)
            both = slice(s * S5_SLAB, (s + 1) * S5_SLAB)
            bu = _dot(u, bmat_ref[z, :, both])
            sr, si = st_ref[z, 0:1, re], st_ref[z, 0:1, im]
            for ci in range(nch):
                r0 = (ci if z == 0 else nch - 1 - ci) * S5_CHUNK
                rows = slice(r0, r0 + S5_CHUNK)
                zr, zi = _cmul(linv_ref[z, :, re], linv_ref[z, :, im], bu[rows, :LANES], bu[rows, LANES:])
                w = _dot(tri_ref[z], jnp.concatenate([zr, zi], axis=1).astype(BF16))
                cr, ci_ = _cmul(lam_ref[z, :, re], lam_ref[z, :, im], sr, si)
                xr, xi = _cmul(lpow_ref[z, :, re], lpow_ref[z, :, im], w[:, :LANES] + cr, w[:, LANES:] + ci_)
                sr, si = xr[edge:edge + 1, :], xi[edge:edge + 1, :]
                x_ref[z, rows, both] = jnp.concatenate([xr, xi], axis=1).astype(BF16)
            st_ref[z, 0:1, re] = sr
            st_ref[z, 0:1, im] = si
        y = _dot(x_ref[z], cmat_ref[z])
        if z == 0:
            y = y + d_ref[...] * u.astype(F32)
        y_ref[...] = y.astype(BF16)


def _s5(u, p, batch, seq_len):
    n = u.shape[0]
    ts = min(TILE["s5"], seq_len)
    per_seq = seq_len // ts
    fwd_map = lambda b, j: (b * per_seq + j, 0)
    bwd_map = lambda b, j: (b * per_seq + per_seq - 1 - j, 0)
    blk = lambda m: pl.BlockSpec((ts, SSM_WIDTH), m)
    return pl.pallas_call(
        functools.partial(_s5_kernel, ts=ts),
        grid=(batch, per_seq),
        in_specs=[blk(fwd_map), blk(bwd_map),
                  _const_spec((2, SSM_WIDTH, 2 * SSM_LANES)), _const_spec((2, 2 * SSM_LANES, SSM_WIDTH)),
                  _const_spec((2, S5_CHUNK, 2 * SSM_LANES)), _const_spec((2, S5_CHUNK, 2 * SSM_LANES)),
                  _const_spec((2, 1, 2 * SSM_LANES)), _const_spec((2, S5_CHUNK, S5_CHUNK)),
                  _const_spec((1, SSM_WIDTH))],
        out_specs=[blk(fwd_map), blk(bwd_map)],
        out_shape=[jax.ShapeDtypeStruct((n, SSM_WIDTH), BF16)] * 2,
        scratch_shapes=[pltpu.VMEM((2, 8, 2 * SSM_LANES), F32), pltpu.VMEM((2, ts, 2 * SSM_LANES), BF16)],
        compiler_params=_cparams("parallel", "arbitrary"),
        name="s5",
    )(u, u, p["bmat"], p["cmat"], p["linv"], p["lpow"], p["lam"], p["tri"], p["d"])


def _layer_norm(x, g, b):
    mu = jnp.mean(x, axis=-1, keepdims=True)
    xc = x - mu
    var = jnp.mean(xc * xc, axis=-1, keepdims=True)
    return xc * lax.rsqrt(var + EPS) * g + b


def _head_mean(x, avg_ref):
    xb = x.astype(BF16)
    return jnp.concatenate([_dot(xb[:, :MXU_DIM], avg_ref[...]),
                            _dot(xb[:, MXU_DIM:], avg_ref[0:LANES, 0:LANES])], axis=1)


def _sigmoid(x):
    return 1.0 / (1.0 + jnp.exp(-x))


def _mix_kernel(x_ref, a_ref, yf_ref, yb_ref, rf_ref, rb_ref, rg_ref, gluw_ref, glub_ref, sg_ref, avg_ref,
                wout_ref, g_ref, b_ref, o_ref):
    y = yf_ref[...].astype(F32) + yb_ref[...].astype(F32)
    s = 0.5 * y * (1.0 + jnp.tanh(math.sqrt(2.0 / math.pi) * (y + 0.044715 * (y * y * y))))
    s = s * _sigmoid(_dot(s.astype(BF16), gluw_ref[...]) + glub_ref[...])
    s = s * lax.rsqrt(jnp.mean(s * s, axis=-1, keepdims=True) + EPS) * sg_ref[...]
    r = rf_ref[...].astype(F32) + rb_ref[...].astype(F32)
    rc = r - _head_mean(r, avg_ref)
    r = rc * lax.rsqrt(_head_mean(rc * rc, avg_ref) + EPS)
    rg = rg_ref[...].astype(F32)
    r = r * (rg * _sigmoid(rg))
    mix = (_dot(a_ref[...], wout_ref[0:ATTN_WIDTH, :])
           + _dot(s.astype(BF16), wout_ref[ATTN_WIDTH:ATTN_WIDTH + SSM_WIDTH, :])
           + _dot(r.astype(BF16), wout_ref[ATTN_WIDTH + SSM_WIDTH:, :]))
    o_ref[...] = _layer_norm(DN_ALPHA * x_ref[...] + mix, g_ref[...], b_ref[...])


def _mix(x, a, yf, yb, rf, rb, rg, lp, avg, seq_len):
    n = x.shape[0]
    tm = min(TILE["mix"], seq_len)
    row = lambda width: pl.BlockSpec((tm, width), lambda i: (i, 0))
    return pl.pallas_call(
        _mix_kernel,
        grid=(n // tm,),
        in_specs=[row(D_MODEL), row(ATTN_WIDTH), row(SSM_WIDTH), row(SSM_WIDTH), row(RET_WIDTH), row(RET_WIDTH),
                  row(RET_WIDTH), _const_spec((SSM_WIDTH, SSM_WIDTH)), _const_spec((1, SSM_WIDTH)),
                  _const_spec((1, SSM_WIDTH)), _const_spec((MXU_DIM, MXU_DIM)),
                  _const_spec((MIX_WIDTH, D_MODEL)), _const_spec((1, D_MODEL)), _const_spec((1, D_MODEL))],
        out_specs=row(D_MODEL),
        out_shape=jax.ShapeDtypeStruct((n, D_MODEL), F32),
        compiler_params=_cparams("parallel"),
        name="mix",
    )(x, a, yf, yb, rf, rb, rg, lp["glu_w"], lp["glu_b"], lp["ssm_g"], avg, lp["w_out"], lp["ln1_g"], lp["ln1_b"])


def _ffn_kernel(x_ref, prev_ref, next_ref, wup_ref, cw_ref, cb_ref, wdn_ref, g_ref, b_ref, o_ref, xe_ref,
                gated_ref, *, seq_len, tm):
    i = pl.program_id(0)
    per_seq = seq_len // tm
    first = (i % per_seq) == 0
    last = (i % per_seq) == per_seq - 1
    halo = BF16_ROWS
    xe_ref[0:halo, :] = jnp.where(first, 0.0, prev_ref[...]).astype(BF16)
    xe_ref[halo:halo + tm, :] = x_ref[...].astype(BF16)
    xe_ref[halo + tm:, :] = jnp.where(last, 0.0, next_ref[...]).astype(BF16)
    rows = lax.broadcasted_iota(jnp.int32, (tm, FF_CHUNK), 0)
    for c in range(D_FF // FF_CHUNK):
        cols = slice(c * FF_CHUNK, (c + 1) * FF_CHUNK)
        val = _dot(xe_ref[halo:halo + tm, :], wup_ref[:, cols])
        act_e = _dot(xe_ref[...], wup_ref[:, D_FF + c * FF_CHUNK:D_FF + (c + 1) * FF_CHUNK])
        act = act_e[halo:halo + tm, :]
        before = jnp.where(rows == 0, act_e[halo - 1:halo, :], pltpu.roll(act, 1, 0))
        after = jnp.where(rows == tm - 1, act_e[halo + tm:halo + tm + 1, :], pltpu.roll(act, tm - 1, 0))
        cv = before * cw_ref[0:1, cols] + act * cw_ref[1:2, cols] + after * cw_ref[2:3, cols] + cb_ref[:, cols]
        gated_ref[:, cols] = (cv * _sigmoid(cv) * val).astype(BF16)
    th = tm // FFN_ROW_SPLIT
    for r in range(FFN_ROW_SPLIT):
        rs = slice(r * th, (r + 1) * th)
        ffn = _dot(gated_ref[rs, :], wdn_ref[...])
        o_ref[rs, :] = _layer_norm(DN_ALPHA * x_ref[rs, :] + ffn, g_ref[...], b_ref[...])


def _ffn(x, lp, seq_len):
    n = x.shape[0]
    tm = min(TILE["ffn"], seq_len)
    halo = BF16_ROWS
    sub = tm // halo
    last = n // halo - 1
    return pl.pallas_call(
        functools.partial(_ffn_kernel, seq_len=seq_len, tm=tm),
        grid=(n // tm,),
        in_specs=[pl.BlockSpec((tm, D_MODEL), lambda i: (i, 0)),
                  pl.BlockSpec((halo, D_MODEL), lambda i: (jnp.maximum(i * sub - 1, 0), 0)),
                  pl.BlockSpec((halo, D_MODEL), lambda i: (jnp.minimum((i + 1) * sub, last), 0)),
                  _const_spec((D_MODEL, 2 * D_FF)), _const_spec((3, D_FF)), _const_spec((1, D_FF)),
                  _const_spec((D_FF, D_MODEL)), _const_spec((1, D_MODEL)), _const_spec((1, D_MODEL))],
        out_specs=pl.BlockSpec((tm, D_MODEL), lambda i: (i, 0)),
        out_shape=jax.ShapeDtypeStruct((n, D_MODEL), F32),
        scratch_shapes=[pltpu.VMEM((tm + 2 * halo, D_MODEL), BF16), pltpu.VMEM((tm, D_FF), BF16)],
        compiler_params=_cparams("parallel"),
        name="ffn",
    )(x, x, x, lp["w_up"], lp["conv_w"], lp["conv_b"], lp["w_down"], lp["ln2_g"], lp["ln2_b"])


def _rotary_tables(seq_len):
    pos = jnp.arange(seq_len, dtype=F32)[:, None]

    def build(inv_freq, scale):
        half = inv_freq.shape[0]
        ang = pos * inv_freq[None, :]
        cos, sin = jnp.cos(ang), jnp.sin(ang)
        pad = HEAD_DIM - 2 * half
        c = jnp.concatenate([cos, cos, jnp.ones((seq_len, pad), F32)], axis=1)
        s_up = jnp.concatenate([-sin, jnp.zeros((seq_len, half + pad), F32)], axis=1)
        s_dn = jnp.concatenate([jnp.zeros((seq_len, half), F32), sin, jnp.zeros((seq_len, pad), F32)], axis=1)
        t = jnp.stack([c, s_up, s_dn]) * scale
        return jnp.tile(t, (1, 1, LANES // HEAD_DIM))

    inv_a = ROPE_THETA ** (-jnp.arange(0, ROPE_DIM, 2, dtype=F32) / ROPE_DIM)
    inv_r = 1.0 / (RET_THETA ** jnp.linspace(0.0, 1.0, HEAD_DIM // 2, dtype=F32))
    scale = HEAD_DIM ** -0.5
    return {"ak": build(inv_a, 1.0), "aq": build(inv_a, scale), "rq": build(inv_r, 1.0), "rk": build(inv_r, scale)}


def _retention_consts():
    c = RET_CHUNK
    log_g = jnp.log(1.0 - 2.0 ** (-5.0 - jnp.arange(RET_HEADS, dtype=F32)))
    idx = jnp.arange(c, dtype=F32)
    decay = jnp.exp(log_g[:, None, None] * jnp.abs(idx[:, None] - idx[None, :]))
    decay = decay.reshape(3, 2, c, c).transpose(0, 2, 1, 3).reshape(3, c, 2 * c)
    per_lane = lambda e: jnp.repeat(jnp.exp(log_g[None, :] * e[:, None]), HEAD_DIM, axis=1)
    gq = jnp.stack([per_lane(idx + 1.0), per_lane(c - idx)])
    gk = jnp.stack([per_lane(c - 1.0 - idx), per_lane(idx)])
    g_chunk = jnp.repeat(jnp.exp(log_g * c), HEAD_DIM).reshape(3, LANES)
    gc = jnp.broadcast_to(g_chunk[:, :, None], (3, LANES, LANES))
    blk = jnp.arange(LANES) // HEAD_DIM
    bd = (blk[:, None] == blk[None, :]).astype(F32)
    seg = jnp.arange(MXU_DIM) // HEAD_DIM
    avg = ((seg[:, None] == seg[None, :]).astype(F32) / HEAD_DIM).astype(BF16)
    return {"decay": decay, "gq": gq, "gk": gk, "gc": gc, "bd": bd}, avg


def _powers(br, bi, count):
    pr, pi = jnp.ones_like(br)[None], jnp.zeros_like(bi)[None]
    sr, si = br, bi
    while pr.shape[0] < count:
        nr, ni = _cmul(pr, pi, sr[None], si[None])
        pr, pi = jnp.concatenate([pr, nr]), jnp.concatenate([pi, ni])
        sr, si = _cmul(sr, si, sr, si)
    return pr[:count], pi[:count]


def _slab(re, im, axis):
    shp = re.shape
    split = shp[:axis] + (S5_SLABS, 1, LANES) + shp[axis + 1:]
    both = jnp.concatenate([re.reshape(split), im.reshape(split)], axis=axis + 1)
    return both.reshape(shp[:axis] + (2 * SSM_LANES,) + shp[axis + 1:])


def _s5_params(lam_re, lam_im, log_dt, b_re, b_im, c_re, c_im, d_skip):
    t = S5_CHUNK
    dt = jnp.exp(log_dt)[:, :, None]
    mag = jnp.exp(lam_re * dt)
    lbr, lbi = mag * jnp.cos(lam_im * dt), mag * jnp.sin(lam_im * dt)
    den = lam_re * lam_re + lam_im * lam_im
    cfr, cfi = _cmul(lbr - 1.0, lbi, lam_re / den, -lam_im / den)
    bbr, bbi = _cmul(cfr[..., None], cfi[..., None], b_re, b_im)
    eye = jnp.eye(SSM_GROUPS, dtype=F32)

    def b_block(m):
        return jnp.einsum('zgph,gk->zghkp', m, eye).reshape(2, SSM_WIDTH, SSM_LANES)

    def c_block(m):
        return jnp.einsum('zghp,gk->zgpkh', m, eye).reshape(2, SSM_LANES, SSM_WIDTH)

    bmat = _slab(b_block(bbr), b_block(bbi), 2).astype(BF16)
    cmat = _slab(c_block(c_re), -c_block(c_im), 1).astype(BF16)
    flat = lambda a: a.reshape(2, SSM_LANES)
    pr, pi = _powers(flat(lbr), flat(lbi), t)
    mag2 = lbr * lbr + lbi * lbi
    qr, qi = _powers(flat(lbr / mag2), flat(-lbi / mag2), t)
    pw = _slab(pr, pi, 2)
    iv = _slab(qr, qi, 2)
    lpow = jnp.stack([pw[:, 0], pw[::-1, 1]])
    linv = jnp.stack([iv[:, 0], iv[::-1, 1]])
    lam = _slab(flat(lbr), flat(lbi), 1)[:, None, :]
    tri = jnp.tril(jnp.ones((t, t), F32))
    tri = jnp.stack([tri, tri.T]).astype(BF16)
    return {"bmat": bmat, "cmat": cmat, "linv": linv, "lpow": lpow, "lam": lam, "tri": tri,
            "d": d_skip.reshape(1, SSM_WIDTH)}


def _trunk(x3, layers, ret_consts, avg):
    batch, seq_len, _ = x3.shape
    x = x3.reshape(batch * seq_len, D_MODEL)
    tabs = _rotary_tables(seq_len)
    for lp in layers:
        kvq, u, rqkv, rg = _in_proj(x, lp["w_in"], tabs, seq_len)
        a = _attention(kvq, lp["sink"], lp["attn_g"], batch, seq_len)
        rf, rb = _retention(rqkv, ret_consts, batch, seq_len)
        yf, yb = _s5(u, lp["s5"], batch, seq_len)
        x = _mix(x, a, yf, yb, rf, rb, rg, lp, avg, seq_len)
        x = _ffn(x, lp, seq_len)
    return x.reshape(batch, seq_len, D_MODEL)


def _prepare_layers(w_in, attn_sink, attn_out_g, lam_re, lam_im, log_dt, b_re, b_im, c_re, c_im, ssm_d, glu_w,
                    glu_b, ssm_out_g, w_out, ln1_g, ln1_b, w_up, conv_w, conv_b, w_down, ln2_g, ln2_b):
    depth = w_in.shape[0]
    heads = [h for c in range(3) for h in (c, 3 + c)]
    head_cols = jnp.concatenate([jnp.arange(h * HEAD_DIM, (h + 1) * HEAD_DIM) for h in heads])
    w_in = jnp.concatenate([w_in[:, :, 384:640], w_in[:, :, head_cols], w_in[:, :, 640:]], axis=2).astype(BF16)
    w_out = jnp.concatenate([w_out[:, head_cols, :], w_out[:, ATTN_WIDTH:, :]], axis=1).astype(BF16)
    attn_out_g = attn_out_g[:, head_cols]
    layers = []
    for l in range(depth):
        layers.append({
            "w_in": w_in[l], "sink": attn_sink[l], "attn_g": attn_out_g[l].reshape(1, ATTN_WIDTH),
            "s5": _s5_params(lam_re[l], lam_im[l], log_dt[l], b_re[l], b_im[l], c_re[l], c_im[l], ssm_d[l]),
            "glu_w": glu_w[l].astype(BF16), "glu_b": glu_b[l].reshape(1, SSM_WIDTH),
            "ssm_g": ssm_out_g[l].reshape(1, SSM_WIDTH), "w_out": w_out[l],
            "ln1_g": ln1_g[l].reshape(1, D_MODEL), "ln1_b": ln1_b[l].reshape(1, D_MODEL),
            "w_up": w_up[l].astype(BF16), "conv_w": conv_w[l], "conv_b": conv_b[l].reshape(1, D_FF),
            "w_down": w_down[l].astype(BF16),
            "ln2_g": ln2_g[l].reshape(1, D_MODEL), "ln2_b": ln2_b[l].reshape(1, D_MODEL),
        })
    return layers


def kernel(x_prompt, x_sample, w_in, attn_sink, attn_out_g, ssm_lambda_re, ssm_lambda_im, ssm_log_dt, ssm_b_re, ssm_b_im, ssm_c_re, ssm_c_im, ssm_d, ssm_glu_w, ssm_glu_b, ssm_out_g, w_out, ln1_g, ln1_b, ffn_w_up, ffn_conv_w, ffn_conv_b, ffn_w_down, ln2_g, ln2_b):
    layers = _prepare_layers(w_in, attn_sink, attn_out_g, ssm_lambda_re, ssm_lambda_im, ssm_log_dt, ssm_b_re,
                             ssm_b_im, ssm_c_re, ssm_c_im, ssm_d, ssm_glu_w, ssm_glu_b, ssm_out_g, w_out, ln1_g,
                             ln1_b, ffn_w_up, ffn_conv_w, ffn_conv_b, ffn_w_down, ln2_g, ln2_b)
    ret_consts, avg = _retention_consts()
    return (_trunk(x_prompt, layers, ret_consts, avg), _trunk(x_sample, layers, ret_consts, avg))
```

```python
import functools
import math

import jax
import jax.numpy as jnp
from jax import lax
from jax.experimental import pallas as pl
from jax.experimental.pallas import tpu as pltpu

F32 = jnp.float32
BF16 = jnp.bfloat16

D_MODEL = 1024
DEPTH = 4
HEAD_DIM = 64
ATTN_WIDTH = 384
ATTN_HEADS = 6
ATTN_KV_HEADS = 2
ATTN_WINDOW = 128
ROPE_THETA = 500000.0
ROPE_DIM = 16
SSM_WIDTH = 256
SSM_GROUP = 16
SSM_GROUPS = 16
SSM_STATE = 64
SSM_LANES = SSM_GROUPS * SSM_STATE
RET_WIDTH = 384
RET_HEADS = 6
RET_CHUNK = 128
RET_THETA = 10000.0
MIX_WIDTH = 1024
IN_WIDTH = 2432
D_FF = 2816
DN_ALPHA = (2 * DEPTH) ** 0.25
EPS = 1e-5

LANES = 128
MXU_DIM = 256
BF16_ROWS = 16
VMEM_LIMIT = 56 * 1024 * 1024

KVQ_WIDTH = 2 * LANES + ATTN_WIDTH
S5_CHUNK = 128
S5_SLAB = 2 * LANES
S5_SLABS = 2 * SSM_LANES // S5_SLAB
FF_CHUNK = 256
FFN_ROW_SPLIT = 2

TILE = {"in_proj": 512, "attn_mix": 512, "retention": 512, "s5": 512, "ffn": 512}

IN_COLUMNS = ([("k", 0), ("v", 0)] + [("q", c) for c in range(3)] + [("u", c) for c in range(2)]
              + [("rq", c) for c in range(3)] + [("rk", c) for c in range(3)] + [("rv", c) for c in range(3)]
              + [("rg", c) for c in range(3)])


def _cparams(*sem):
    return pltpu.CompilerParams(dimension_semantics=sem, vmem_limit_bytes=VMEM_LIMIT)


def _const_spec(shape):
    nd = len(shape)
    return pl.BlockSpec(shape, lambda *_: (0,) * nd)


def _dot(a, b):
    return jnp.dot(a, b, preferred_element_type=F32)


def _dot_nt(a, b):
    return lax.dot_general(a, b, (((1,), (1,)), ((), ())), preferred_element_type=F32)


def _dot_tn(a, b):
    return lax.dot_general(a, b, (((0,), (0,)), ((), ())), preferred_element_type=F32)


def _rotate(xc, tab_ref, shift):
    up = pltpu.roll(xc, LANES - shift, 1)
    dn = pltpu.roll(xc, shift, 1)
    return xc * tab_ref[0] + up * tab_ref[1] + dn * tab_ref[2]


def _in_proj_kernel(x_ref, w_ref, ak_ref, aq_ref, rq_ref, rk_ref, kvq_ref, u_ref, rqkv_ref, rg_ref):
    xb = x_ref[...].astype(BF16)
    roles = {
        "k": (kvq_ref, 0, ak_ref, ROPE_DIM // 2), "v": (kvq_ref, LANES, None, 0),
        "q": (kvq_ref, 2 * LANES, aq_ref, ROPE_DIM // 2), "u": (u_ref, 0, None, 0),
        "rq": (rqkv_ref, 0, rq_ref, HEAD_DIM // 2), "rk": (rqkv_ref, RET_WIDTH, rk_ref, HEAD_DIM // 2),
        "rv": (rqkv_ref, 2 * RET_WIDTH, None, 0), "rg": (rg_ref, 0, None, 0),
    }
    for c0 in range(0, IN_WIDTH, MXU_DIM):
        width = min(MXU_DIM, IN_WIDTH - c0)
        h = _dot(xb, w_ref[:, c0:c0 + width])
        for j in range(width // LANES):
            role, col = IN_COLUMNS[c0 // LANES + j]
            dst, base, tab, shift = roles[role]
            hc = h[:, j * LANES:(j + 1) * LANES]
            if tab is not None:
                hc = _rotate(hc, tab, shift)
            dst[:, base + col * LANES:base + (col + 1) * LANES] = hc.astype(BF16)


def _in_proj(x, w, tabs, seq_len):
    n = x.shape[0]
    tm = min(TILE["in_proj"], seq_len)
    per_seq = seq_len // tm
    tab_spec = pl.BlockSpec((3, tm, LANES), lambda i: (0, i % per_seq, 0))
    row = lambda width: pl.BlockSpec((tm, width), lambda i: (i, 0))
    widths = (KVQ_WIDTH, SSM_WIDTH, 3 * RET_WIDTH, RET_WIDTH)
    return pl.pallas_call(
        _in_proj_kernel,
        grid=(n // tm,),
        in_specs=[row(D_MODEL), _const_spec((D_MODEL, IN_WIDTH)), tab_spec, tab_spec, tab_spec, tab_spec],
        out_specs=[row(wd) for wd in widths],
        out_shape=[jax.ShapeDtypeStruct((n, wd), BF16) for wd in widths],
        compiler_params=_cparams("parallel"),
        name="in_proj",
    )(x, w, tabs["ak"], tabs["aq"], tabs["rq"], tabs["rk"])


def _attn_kernel(sink_ref, main_ref, prev_ref, next_ref, g_ref, o_ref, *, seq_len, tq):
    i = pl.program_id(1)
    w = ATTN_WINDOW
    kv_all = jnp.concatenate([prev_ref[...], main_ref[:, 0:2 * LANES], next_ref[...]], axis=0).astype(F32)
    low = (lax.broadcasted_iota(jnp.int32, kv_all.shape, 1) & (LANES - 1)) < HEAD_DIM
    zero = jnp.zeros_like(kv_all)
    per_kv = (jnp.where(low, kv_all, zero).astype(BF16), jnp.where(low, zero, kv_all).astype(BF16))

    qi = lax.broadcasted_iota(jnp.int32, (w, w), 0)
    kj = lax.broadcasted_iota(jnp.int32, (w, w), 1)
    g = g_ref[...]

    for sb in range(tq // w):
        r0 = sb * w
        pos0 = i * tq + r0
        left_ok = kj >= qi + jnp.where(pos0 >= w, 0, w)
        right_ok = kj <= qi - jnp.where(pos0 + w < seq_len, 0, w)
        q3 = jnp.concatenate([main_ref[r0:r0 + w, (2 + c) * LANES:(3 + c) * LANES] for c in range(3)], axis=0)
        acc = None
        for kvh in range(ATTN_KV_HEADS):
            keys = per_kv[kvh][r0:r0 + 3 * w, 0:LANES]
            vals = per_kv[kvh][r0:r0 + 3 * w, LANES:2 * LANES]
            s = _dot_nt(q3, keys)
            probs, invs = [], []
            for c in range(3):
                sk = sink_ref[kvh * 3 + c]
                sc = s[c * w:(c + 1) * w, :]
                s_l = jnp.where(left_ok, sc[:, 0:w], -1e30)
                s_m = sc[:, w:2 * w]
                s_r = jnp.where(right_ok, sc[:, 2 * w:], -1e30)
                m = jnp.maximum(jnp.max(jnp.maximum(jnp.maximum(s_l, s_m), s_r), axis=-1, keepdims=True), sk)
                p_l, p_m, p_r = jnp.exp(s_l - m), jnp.exp(s_m - m), jnp.exp(s_r - m)
                den = jnp.sum(p_l + p_m + p_r, axis=-1, keepdims=True) + jnp.exp(sk - m)
                invs.append(1.0 / den)
                probs.append(jnp.concatenate([p_l, p_m, p_r], axis=1).astype(BF16))
            pv = _dot(jnp.concatenate(probs, axis=0), vals) * jnp.concatenate(invs, axis=0)
            acc = pv if acc is None else acc + pv
        a = jnp.concatenate([acc[0:w], acc[w:2 * w], acc[2 * w:]], axis=1)
        a = a * lax.rsqrt(jnp.mean(a * a, axis=-1, keepdims=True) + EPS) * g
        o_ref[r0:r0 + w, :] = a.astype(BF16)


def _ret_kernel(fwd_ref, bwd_ref, decay_ref, gq_ref, gk_ref, gc_ref, bd_ref, rf_ref, rb_ref, sf_ref, sb_ref, *, tr):
    j = pl.program_id(1)

    @pl.when(j == 0)
    def _():
        sf_ref[...] = jnp.zeros_like(sf_ref)
        sb_ref[...] = jnp.zeros_like(sb_ref)

    nch = tr // RET_CHUNK
    half = lax.broadcasted_iota(jnp.int32, (RET_CHUNK, LANES), 1) < HEAD_DIM
    zero = jnp.zeros((RET_CHUNK, LANES), F32)
    bd = bd_ref[...]

    def parts(ref, r0, c):
        col = lambda base: ref[r0:r0 + RET_CHUNK, base + c * LANES:base + (c + 1) * LANES]
        return col(0), col(RET_WIDTH), col(2 * RET_WIDTH)

    for ci in range(nch):
        r0 = ci * RET_CHUNK
        for c in range(3):
            lanes = slice(c * LANES, (c + 1) * LANES)
            q, k, v = parts(fwd_ref, r0, c)
            kf, vf = k.astype(F32), v.astype(F32)
            k2 = jnp.concatenate([jnp.where(half, kf, zero), jnp.where(half, zero, kf)], axis=0).astype(BF16)
            v2 = jnp.concatenate([jnp.where(half, vf, zero), jnp.where(half, zero, vf)], axis=0).astype(BF16)
            s = _dot_nt(q, k2) * decay_ref[c]
            qf = (q.astype(F32) * gq_ref[0, :, lanes]).astype(BF16)
            lhs = jnp.concatenate([s.astype(BF16), qf], axis=1)
            rhs = jnp.concatenate([v2, sf_ref[c].astype(BF16)], axis=0)
            rf_ref[r0:r0 + RET_CHUNK, lanes] = _dot(lhs, rhs).astype(BF16)
            kd = (kf * gk_ref[0, :, lanes]).astype(BF16)
            sf_ref[c] = gc_ref[c] * sf_ref[c] + _dot_tn(kd, v) * bd
        r0 = (nch - 1 - ci) * RET_CHUNK
        for c in range(3):
            lanes = slice(c * LANES, (c + 1) * LANES)
            q, k, v = parts(bwd_ref, r0, c)
            qb = (q.astype(F32) * gq_ref[1, :, lanes]).astype(BF16)
            rb_ref[r0:r0 + RET_CHUNK, lanes] = _dot(qb, sb_ref[c].astype(BF16)).astype(BF16)
            kd = (k.astype(F32) * gk_ref[1, :, lanes]).astype(BF16)
            sb_ref[c] = gc_ref[c] * sb_ref[c] + _dot_tn(kd, v) * bd


def _retention(rqkv, consts, batch, seq_len):
    n = rqkv.shape[0]
    tr = min(TILE["retention"], seq_len)
    per_seq = seq_len // tr
    fwd_map = lambda b, j: (b * per_seq + j, 0)
    bwd_map = lambda b, j: (b * per_seq + per_seq - 1 - j, 0)
    return pl.pallas_call(
        functools.partial(_ret_kernel, tr=tr),
        grid=(batch, per_seq),
        in_specs=[pl.BlockSpec((tr, 3 * RET_WIDTH), fwd_map), pl.BlockSpec((tr, 3 * RET_WIDTH), bwd_map),
                  _const_spec((3, RET_CHUNK, 2 * RET_CHUNK)), _const_spec((2, RET_CHUNK, RET_WIDTH)),
                  _const_spec((2, RET_CHUNK, RET_WIDTH)), _const_spec((3, LANES, LANES)),
                  _const_spec((LANES, LANES))],
        out_specs=[pl.BlockSpec((tr, RET_WIDTH), fwd_map), pl.BlockSpec((tr, RET_WIDTH), bwd_map)],
        out_shape=[jax.ShapeDtypeStruct((n, RET_WIDTH), BF16)] * 2,
        scratch_shapes=[pltpu.VMEM((3, LANES, LANES), F32), pltpu.VMEM((3, LANES, LANES), F32)],
        compiler_params=_cparams("parallel", "arbitrary"),
        name="retention",
    )(rqkv, rqkv, consts["decay"], consts["gq"], consts["gk"], consts["gc"], consts["bd"])


def _cmul(ar, ai, br, bi):
    return ar * br - ai * bi, ar * bi + ai * br


def _s5_kernel(uf_ref, ub_ref, bmat_ref, cmat_ref, linv_ref, lpow_ref, lam_ref, tri_ref, d_ref,
               yf_ref, yb_ref, st_ref, x_ref, *, ts):
    j = pl.program_id(1)

    @pl.when(j == 0)
    def _():
        st_ref[...] = jnp.zeros_like(st_ref)

    nch = ts // S5_CHUNK
    for z, (u_ref, y_ref) in enumerate(((uf_ref, yf_ref), (ub_ref, yb_ref))):
        edge = S5_CHUNK - 1 if z == 0 else 0
        u = u_ref[...]
        for s in range(S5_SLABS):
            re = slice(s * S5_SLAB, s * S5_SLAB + LANES)
            im = slice(s * S5_SLAB + LANES, (s + 1) * S5_SLAB)
            both = slice(s * S5_SLAB, (s + 1) * S5_SLAB)
            bu = _dot(u, bmat_ref[z, :, both])
            sr, si = st_ref[z, 0:1, re], st_ref[z, 0:1, im]
            for ci in range(nch):
                r0 = (ci if z == 0 else nch - 1 - ci) * S5_CHUNK
                rows = slice(r0, r0 + S5_CHUNK)
                zr, zi = _cmul(linv_ref[z, :, re], linv_ref[z, :, im], bu[rows, :LANES], bu[rows, LANES:])
                w = _dot(tri_ref[z], jnp.concatenate([zr, zi], axis=1).astype(BF16))
                cr, ci_ = _cmul(lam_ref[z, :, re], lam_ref[z, :, im], sr, si)
                xr, xi = _cmul(lpow_ref[z, :, re], lpow_ref[z, :, im], w[:, :LANES] + cr, w[:, LANES:] + ci_)
                sr, si = xr[edge:edge + 1, :], xi[edge:edge + 1, :]
                x_ref[z, rows, both] = jnp.concatenate([xr, xi], axis=1).astype(BF16)
            st_ref[z, 0:1, re] = sr
            st_ref[z, 0:1, im] = si
        y = _dot(x_ref[z], cmat_ref[z])
        if z == 0:
            y = y + d_ref[...] * u.astype(F32)
        y_ref[...] = y.astype(BF16)


def _s5(u, p, batch, seq_len):
    n = u.shape[0]
    ts = min(TILE["s5"], seq_len)
    per_seq = seq_len // ts
    fwd_map = lambda b, j: (b * per_seq + j, 0)
    bwd_map = lambda b, j: (b * per_seq + per_seq - 1 - j, 0)
    blk = lambda m: pl.BlockSpec((ts, SSM_WIDTH), m)
    return pl.pallas_call(
        functools.partial(_s5_kernel, ts=ts),
        grid=(batch, per_seq),
        in_specs=[blk(fwd_map), blk(bwd_map),
                  _const_spec((2, SSM_WIDTH, 2 * SSM_LANES)), _const_spec((2, 2 * SSM_LANES, SSM_WIDTH)),
                  _const_spec((2, S5_CHUNK, 2 * SSM_LANES)), _const_spec((2, S5_CHUNK, 2 * SSM_LANES)),
                  _const_spec((2, 1, 2 * SSM_LANES)), _const_spec((2, S5_CHUNK, S5_CHUNK)),
                  _const_spec((1, SSM_WIDTH))],
        out_specs=[blk(fwd_map), blk(bwd_map)],
        out_shape=[jax.ShapeDtypeStruct((n, SSM_WIDTH), BF16)] * 2,
        scratch_shapes=[pltpu.VMEM((2, 8, 2 * SSM_LANES), F32), pltpu.VMEM((2, ts, 2 * SSM_LANES), BF16)],
        compiler_params=_cparams("parallel", "arbitrary"),
        name="s5",
    )(u, u, p["bmat"], p["cmat"], p["linv"], p["lpow"], p["lam"], p["tri"], p["d"])


def _layer_norm(x, g, b):
    mu = jnp.mean(x, axis=-1, keepdims=True)
    xc = x - mu
    var = jnp.mean(xc * xc, axis=-1, keepdims=True)
    return xc * lax.rsqrt(var + EPS) * g + b


def _head_mean(x, avg_ref):
    xb = x.astype(BF16)
    return jnp.concatenate([_dot(xb[:, :MXU_DIM], avg_ref[...]),
                            _dot(xb[:, MXU_DIM:], avg_ref[0:LANES, 0:LANES])], axis=1)


def _sigmoid(x):
    return 1.0 / (1.0 + jnp.exp(-x))


def _mix_kernel(x_ref, a_ref, yf_ref, yb_ref, rf_ref, rb_ref, rg_ref, gluw_ref, glub_ref, sg_ref, avg_ref,
                wout_ref, g_ref, b_ref, o_ref):
    y = yf_ref[...].astype(F32) + yb_ref[...].astype(F32)
    s = 0.5 * y * (1.0 + jnp.tanh(math.sqrt(2.0 / math.pi) * (y + 0.044715 * (y * y * y))))
    s = s * _sigmoid(_dot(s.astype(BF16), gluw_ref[...]) + glub_ref[...])
    s = s * lax.rsqrt(jnp.mean(s * s, axis=-1, keepdims=True) + EPS) * sg_ref[...]
    r = rf_ref[...].astype(F32) + rb_ref[...].astype(F32)
    rc = r - _head_mean(r, avg_ref)
    r = rc * lax.rsqrt(_head_mean(rc * rc, avg_ref) + EPS)
    rg = rg_ref[...].astype(F32)
    r = r * (rg * _sigmoid(rg))
    mix = (_dot(a_ref[...], wout_ref[0:ATTN_WIDTH, :])
           + _dot(s.astype(BF16), wout_ref[ATTN_WIDTH:ATTN_WIDTH + SSM_WIDTH, :])
           + _dot(r.astype(BF16), wout_ref[ATTN_WIDTH + SSM_WIDTH:, :]))
    o_ref[...] = _layer_norm(DN_ALPHA * x_ref[...] + mix, g_ref[...], b_ref[...])


def _attn_mix_kernel(sink_ref, main_ref, prev_ref, next_ref, ag_ref, x_ref, yf_ref, yb_ref, rf_ref, rb_ref, rg_ref,
                     gluw_ref, glub_ref, sg_ref, avg_ref, wout_ref, g_ref, b_ref, o_ref, a_ref, *, seq_len, tq):
    _attn_kernel(sink_ref, main_ref, prev_ref, next_ref, ag_ref, a_ref, seq_len=seq_len, tq=tq)
    _mix_kernel(x_ref, a_ref, yf_ref, yb_ref, rf_ref, rb_ref, rg_ref, gluw_ref, glub_ref, sg_ref, avg_ref,
                wout_ref, g_ref, b_ref, o_ref)


def _attn_mix(x, kvq, yf, yb, rf, rb, rg, lp, avg, batch, seq_len):
    n = x.shape[0]
    tq = min(TILE["attn_mix"], seq_len)
    per_seq = seq_len // tq
    sub = tq // ATTN_WINDOW
    last = n // ATTN_WINDOW - 1
    row = lambda width: pl.BlockSpec((tq, width), lambda b, i: (b * per_seq + i, 0))
    prev = pl.BlockSpec((ATTN_WINDOW, 2 * LANES), lambda b, i: (jnp.maximum((b * per_seq + i) * sub - 1, 0), 0))
    nxt = pl.BlockSpec((ATTN_WINDOW, 2 * LANES), lambda b, i: (jnp.minimum((b * per_seq + i + 1) * sub, last), 0))
    return pl.pallas_call(
        functools.partial(_attn_mix_kernel, seq_len=seq_len, tq=tq),
        grid=(batch, per_seq),
        in_specs=[pl.BlockSpec(memory_space=pltpu.SMEM), row(KVQ_WIDTH), prev, nxt, _const_spec((1, ATTN_WIDTH)),
                  row(D_MODEL), row(SSM_WIDTH), row(SSM_WIDTH), row(RET_WIDTH), row(RET_WIDTH), row(RET_WIDTH),
                  _const_spec((SSM_WIDTH, SSM_WIDTH)), _const_spec((1, SSM_WIDTH)), _const_spec((1, SSM_WIDTH)),
                  _const_spec((MXU_DIM, MXU_DIM)), _const_spec((MIX_WIDTH, D_MODEL)), _const_spec((1, D_MODEL)),
                  _const_spec((1, D_MODEL))],
        out_specs=row(D_MODEL),
        out_shape=jax.ShapeDtypeStruct((n, D_MODEL), F32),
        scratch_shapes=[pltpu.VMEM((tq, ATTN_WIDTH), BF16)],
        compiler_params=_cparams("parallel", "parallel"),
        name="attn_mix",
    )(lp["sink"], kvq, kvq, kvq, lp["attn_g"], x, yf, yb, rf, rb, rg, lp["glu_w"], lp["glu_b"], lp["ssm_g"], avg,
      lp["w_out"], lp["ln1_g"], lp["ln1_b"])


def _ffn_kernel(x_ref, prev_ref, next_ref, wup_ref, cw_ref, cb_ref, wdn_ref, g_ref, b_ref, o_ref, xe_ref,
                gated_ref, *, seq_len, tm):
    i = pl.program_id(0)
    per_seq = seq_len // tm
    first = (i % per_seq) == 0
    last = (i % per_seq) == per_seq - 1
    halo = BF16_ROWS
    xe_ref[0:halo, :] = jnp.where(first, 0.0, prev_ref[...]).astype(BF16)
    xe_ref[halo:halo + tm, :] = x_ref[...].astype(BF16)
    xe_ref[halo + tm:, :] = jnp.where(last, 0.0, next_ref[...]).astype(BF16)
    rows = lax.broadcasted_iota(jnp.int32, (tm, FF_CHUNK), 0)
    for c in range(D_FF // FF_CHUNK):
        cols = slice(c * FF_CHUNK, (c + 1) * FF_CHUNK)
        val = _dot(xe_ref[halo:halo + tm, :], wup_ref[:, cols])
        act_e = _dot(xe_ref[...], wup_ref[:, D_FF + c * FF_CHUNK:D_FF + (c + 1) * FF_CHUNK])
        act = act_e[halo:halo + tm, :]
        before = jnp.where(rows == 0, act_e[halo - 1:halo, :], pltpu.roll(act, 1, 0))
        after = jnp.where(rows == tm - 1, act_e[halo + tm:halo + tm + 1, :], pltpu.roll(act, tm - 1, 0))
        cv = before * cw_ref[0:1, cols] + act * cw_ref[1:2, cols] + after * cw_ref[2:3, cols] + cb_ref[:, cols]
        gated_ref[:, cols] = (cv * _sigmoid(cv) * val).astype(BF16)
    th = tm // FFN_ROW_SPLIT
    for r in range(FFN_ROW_SPLIT):
        rs = slice(r * th, (r + 1) * th)
        ffn = _dot(gated_ref[rs, :], wdn_ref[...])
        o_ref[rs, :] = _layer_norm(DN_ALPHA * x_ref[rs, :] + ffn, g_ref[...], b_ref[...])


def _ffn(x, lp, seq_len):
    n = x.shape[0]
    tm = min(TILE["ffn"], seq_len)
    halo = BF16_ROWS
    sub = tm // halo
    last = n // halo - 1
    return pl.pallas_call(
        functools.partial(_ffn_kernel, seq_len=seq_len, tm=tm),
        grid=(n // tm,),
        in_specs=[pl.BlockSpec((tm, D_MODEL), lambda i: (i, 0)),
                  pl.BlockSpec((halo, D_MODEL), lambda i: (jnp.maximum(i * sub - 1, 0), 0)),
                  pl.BlockSpec((halo, D_MODEL), lambda i: (jnp.minimum((i + 1) * sub, last), 0)),
                  _const_spec((D_MODEL, 2 * D_FF)), _const_spec((3, D_FF)), _const_spec((1, D_FF)),
                  _const_spec((D_FF, D_MODEL)), _const_spec((1, D_MODEL)), _const_spec((1, D_MODEL))],
        out_specs=pl.BlockSpec((tm, D_MODEL), lambda i: (i, 0)),
        out_shape=jax.ShapeDtypeStruct((n, D_MODEL), F32),
        scratch_shapes=[pltpu.VMEM((tm + 2 * halo, D_MODEL), BF16), pltpu.VMEM((tm, D_FF), BF16)],
        compiler_params=_cparams("parallel"),
        name="ffn",
    )(x, x, x, lp["w_up"], lp["conv_w"], lp["conv_b"], lp["w_down"], lp["ln2_g"], lp["ln2_b"])


def _rotary_tables(seq_len):
    pos = jnp.arange(seq_len, dtype=F32)[:, None]

    def build(inv_freq, scale):
        half = inv_freq.shape[0]
        ang = pos * inv_freq[None, :]
        cos, sin = jnp.cos(ang), jnp.sin(ang)
        pad = HEAD_DIM - 2 * half
        c = jnp.concatenate([cos, cos, jnp.ones((seq_len, pad), F32)], axis=1)
        s_up = jnp.concatenate([-sin, jnp.zeros((seq_len, half + pad), F32)], axis=1)
        s_dn = jnp.concatenate([jnp.zeros((seq_len, half), F32), sin, jnp.zeros((seq_len, pad), F32)], axis=1)
        t = jnp.stack([c, s_up, s_dn]) * scale
        return jnp.tile(t, (1, 1, LANES // HEAD_DIM))

    inv_a = ROPE_THETA ** (-jnp.arange(0, ROPE_DIM, 2, dtype=F32) / ROPE_DIM)
    inv_r = 1.0 / (RET_THETA ** jnp.linspace(0.0, 1.0, HEAD_DIM // 2, dtype=F32))
    scale = HEAD_DIM ** -0.5
    return {"ak": build(inv_a, 1.0), "aq": build(inv_a, scale), "rq": build(inv_r, 1.0), "rk": build(inv_r, scale)}


def _retention_consts():
    c = RET_CHUNK
    log_g = jnp.log(1.0 - 2.0 ** (-5.0 - jnp.arange(RET_HEADS, dtype=F32)))
    idx = jnp.arange(c, dtype=F32)
    decay = jnp.exp(log_g[:, None, None] * jnp.abs(idx[:, None] - idx[None, :]))
    decay = decay.reshape(3, 2, c, c).transpose(0, 2, 1, 3).reshape(3, c, 2 * c)
    per_lane = lambda e: jnp.repeat(jnp.exp(log_g[None, :] * e[:, None]), HEAD_DIM, axis=1)
    gq = jnp.stack([per_lane(idx + 1.0), per_lane(c - idx)])
    gk = jnp.stack([per_lane(c - 1.0 - idx), per_lane(idx)])
    g_chunk = jnp.repeat(jnp.exp(log_g * c), HEAD_DIM).reshape(3, LANES)
    gc = jnp.broadcast_to(g_chunk[:, :, None], (3, LANES, LANES))
    blk = jnp.arange(LANES) // HEAD_DIM
    bd = (blk[:, None] == blk[None, :]).astype(F32)
    seg = jnp.arange(MXU_DIM) // HEAD_DIM
    avg = ((seg[:, None] == seg[None, :]).astype(F32) / HEAD_DIM).astype(BF16)
    return {"decay": decay, "gq": gq, "gk": gk, "gc": gc, "bd": bd}, avg


def _powers(br, bi, count):
    pr, pi = jnp.ones_like(br)[None], jnp.zeros_like(bi)[None]
    sr, si = br, bi
    while pr.shape[0] < count:
        nr, ni = _cmul(pr, pi, sr[None], si[None])
        pr, pi = jnp.concatenate([pr, nr]), jnp.concatenate([pi, ni])
        sr, si = _cmul(sr, si, sr, si)
    return pr[:count], pi[:count]


def _slab(re, im, axis):
    shp = re.shape
    split = shp[:axis] + (S5_SLABS, 1, LANES) + shp[axis + 1:]
    both = jnp.concatenate([re.reshape(split), im.reshape(split)], axis=axis + 1)
    return both.reshape(shp[:axis] + (2 * SSM_LANES,) + shp[axis + 1:])


def _s5_params(lam_re, lam_im, log_dt, b_re, b_im, c_re, c_im, d_skip):
    t = S5_CHUNK
    dt = jnp.exp(log_dt)[:, :, None]
    mag = jnp.exp(lam_re * dt)
    lbr, lbi = mag * jnp.cos(lam_im * dt), mag * jnp.sin(lam_im * dt)
    den = lam_re * lam_re + lam_im * lam_im
    cfr, cfi = _cmul(lbr - 1.0, lbi, lam_re / den, -lam_im / den)
    bbr, bbi = _cmul(cfr[..., None], cfi[..., None], b_re, b_im)
    eye = jnp.eye(SSM_GROUPS, dtype=F32)

    def b_block(m):
        return jnp.einsum('zgph,gk->zghkp', m, eye).reshape(2, SSM_WIDTH, SSM_LANES)

    def c_block(m):
        return jnp.einsum('zghp,gk->zgpkh', m, eye).reshape(2, SSM_LANES, SSM_WIDTH)

    bmat = _slab(b_block(bbr), b_block(bbi), 2).astype(BF16)
    cmat = _slab(c_block(c_re), -c_block(c_im), 1).astype(BF16)
    flat = lambda a: a.reshape(2, SSM_LANES)
    pr, pi = _powers(flat(lbr), flat(lbi), t)
    mag2 = lbr * lbr + lbi * lbi
    qr, qi = _powers(flat(lbr / mag2), flat(-lbi / mag2), t)
    pw = _slab(pr, pi, 2)
    iv = _slab(qr, qi, 2)
    lpow = jnp.stack([pw[:, 0], pw[::-1, 1]])
    linv = jnp.stack([iv[:, 0], iv[::-1, 1]])
    lam = _slab(flat(lbr), flat(lbi), 1)[:, None, :]
    tri = jnp.tril(jnp.ones((t, t), F32))
    tri = jnp.stack([tri, tri.T]).astype(BF16)
    return {"bmat": bmat, "cmat": cmat, "linv": linv, "lpow": lpow, "lam": lam, "tri": tri,
            "d": d_skip.reshape(1, SSM_WIDTH)}


def _trunk(x3, layers, ret_consts, avg):
    batch, seq_len, _ = x3.shape
    x = x3.reshape(batch * seq_len, D_MODEL)
    tabs = _rotary_tables(seq_len)
    for lp in layers:
        kvq, u, rqkv, rg = _in_proj(x, lp["w_in"], tabs, seq_len)
        rf, rb = _retention(rqkv, ret_consts, batch, seq_len)
        yf, yb = _s5(u, lp["s5"], batch, seq_len)
        x = _attn_mix(x, kvq, yf, yb, rf, rb, rg, lp, avg, batch, seq_len)
        x = _ffn(x, lp, seq_len)
    return x.reshape(batch, seq_len, D_MODEL)


def _prepare_layers(w_in, attn_sink, attn_out_g, lam_re, lam_im, log_dt, b_re, b_im, c_re, c_im, ssm_d, glu_w,
                    glu_b, ssm_out_g, w_out, ln1_g, ln1_b, w_up, conv_w, conv_b, w_down, ln2_g, ln2_b):
    depth = w_in.shape[0]
    heads = [h for c in range(3) for h in (c, 3 + c)]
    regroup = lambda a, axis: [lax.slice_in_dim(a, h * HEAD_DIM, (h + 1) * HEAD_DIM, axis=axis) for h in heads]
    w_in = jnp.concatenate([w_in[:, :, 384:640]] + regroup(w_in, 2) + [w_in[:, :, 640:]], axis=2).astype(BF16)
    w_out = jnp.concatenate(regroup(w_out, 1) + [w_out[:, ATTN_WIDTH:, :]], axis=1).astype(BF16)
    attn_out_g = jnp.concatenate(regroup(attn_out_g, 1), axis=1)
    layers = []
    for l in range(depth):
        layers.append({
            "w_in": w_in[l], "sink": attn_sink[l], "attn_g": attn_out_g[l].reshape(1, ATTN_WIDTH),
            "s5": _s5_params(lam_re[l], lam_im[l], log_dt[l], b_re[l], b_im[l], c_re[l], c_im[l], ssm_d[l]),
            "glu_w": glu_w[l].astype(BF16), "glu_b": glu_b[l].reshape(1, SSM_WIDTH),
            "ssm_g": ssm_out_g[l].reshape(1, SSM_WIDTH), "w_out": w_out[l],
            "ln1_g": ln1_g[l].reshape(1, D_MODEL), "ln1_b": ln1_b[l].reshape(1, D_MODEL),
            "w_up": w_up[l].astype(BF16), "conv_w": conv_w[l], "conv_b": conv_b[l].reshape(1, D_FF),
            "w_down": w_down[l].astype(BF16),
            "ln2_g": ln2_g[l].reshape(1, D_MODEL), "ln2_b": ln2_b[l].reshape(1, D_MODEL),
        })
    return layers


def kernel(x_prompt, x_sample, w_in, attn_sink, attn_out_g, ssm_lambda_re, ssm_lambda_im, ssm_log_dt, ssm_b_re, ssm_b_im, ssm_c_re, ssm_c_im, ssm_d, ssm_glu_w, ssm_glu_b, ssm_out_g, w_out, ln1_g, ln1_b, ffn_w_up, ffn_conv_w, ffn_conv_b, ffn_w_down, ln2_g, ln2_b):
    layers = _prepare_layers(w_in, attn_sink, attn_out_g, ssm_lambda_re, ssm_lambda_im, ssm_log_dt, ssm_b_re,
                             ssm_b_im, ssm_c_re, ssm_c_im, ssm_d, ssm_glu_w, ssm_glu_b, ssm_out_g, w_out, ln1_g,
                             ln1_b, ffn_w_up, ffn_conv_w, ffn_conv_b, ffn_w_down, ln2_g, ln2_b)
    ret_consts, avg = _retention_consts()
    return (_trunk(x_prompt, layers, ret_consts, avg), _trunk(x_sample, layers, ret_consts, avg))
```

```python
import functools
import math

import jax
import jax.numpy as jnp
from jax import lax
from jax.experimental import pallas as pl
from jax.experimental.pallas import tpu as pltpu

F32 = jnp.float32
BF16 = jnp.bfloat16

D_MODEL = 1024
DEPTH = 4
HEAD_DIM = 64
ATTN_WIDTH = 384
ATTN_HEADS = 6
ATTN_KV_HEADS = 2
ATTN_WINDOW = 128
ROPE_THETA = 500000.0
ROPE_DIM = 16
SSM_WIDTH = 256
SSM_GROUP = 16
SSM_GROUPS = 16
SSM_STATE = 64
SSM_LANES = SSM_GROUPS * SSM_STATE
RET_WIDTH = 384
RET_HEADS = 6
RET_CHUNK = 128
RET_THETA = 10000.0
MIX_WIDTH = 1024
IN_WIDTH = 2432
D_FF = 2816
DN_ALPHA = (2 * DEPTH) ** 0.25
EPS = 1e-5

LANES = 128
MXU_DIM = 256
BF16_ROWS = 16
VMEM_LIMIT = 56 * 1024 * 1024

KVQ_WIDTH = 2 * LANES + ATTN_WIDTH
FF_CHUNK = 256
FFN_ROW_SPLIT = 2

TILE = {"in_proj": 512, "attn_mix": 512, "retention": 512, "s5": 4096, "ffn": 512}

IN_COLUMNS = ([("k", 0), ("v", 0)] + [("q", c) for c in range(3)] + [("u", c) for c in range(2)]
              + [("rq", c) for c in range(3)] + [("rk", c) for c in range(3)] + [("rv", c) for c in range(3)]
              + [("rg", c) for c in range(3)])


def _cparams(*sem):
    return pltpu.CompilerParams(dimension_semantics=sem, vmem_limit_bytes=VMEM_LIMIT)


def _const_spec(shape):
    nd = len(shape)
    return pl.BlockSpec(shape, lambda *_: (0,) * nd)


def _dot(a, b):
    return jnp.dot(a, b, preferred_element_type=F32)


def _dot_nt(a, b):
    return lax.dot_general(a, b, (((1,), (1,)), ((), ())), preferred_element_type=F32)


def _dot_tn(a, b):
    return lax.dot_general(a, b, (((0,), (0,)), ((), ())), preferred_element_type=F32)


def _rotate(xc, tab_ref, shift):
    up = pltpu.roll(xc, LANES - shift, 1)
    dn = pltpu.roll(xc, shift, 1)
    return xc * tab_ref[0] + up * tab_ref[1] + dn * tab_ref[2]


def _in_proj_kernel(x_ref, w_ref, ak_ref, aq_ref, rq_ref, rk_ref, kvq_ref, u_ref, rqkv_ref, rg_ref):
    xb = x_ref[...].astype(BF16)
    roles = {
        "k": (kvq_ref, 0, ak_ref, ROPE_DIM // 2), "v": (kvq_ref, LANES, None, 0),
        "q": (kvq_ref, 2 * LANES, aq_ref, ROPE_DIM // 2), "u": (u_ref, 0, None, 0),
        "rq": (rqkv_ref, 0, rq_ref, HEAD_DIM // 2), "rk": (rqkv_ref, RET_WIDTH, rk_ref, HEAD_DIM // 2),
        "rv": (rqkv_ref, 2 * RET_WIDTH, None, 0), "rg": (rg_ref, 0, None, 0),
    }
    for c0 in range(0, IN_WIDTH, MXU_DIM):
        width = min(MXU_DIM, IN_WIDTH - c0)
        h = _dot(xb, w_ref[:, c0:c0 + width])
        for j in range(width // LANES):
            role, col = IN_COLUMNS[c0 // LANES + j]
            dst, base, tab, shift = roles[role]
            hc = h[:, j * LANES:(j + 1) * LANES]
            if tab is not None:
                hc = _rotate(hc, tab, shift)
            dst[:, base + col * LANES:base + (col + 1) * LANES] = hc.astype(BF16)


def _in_proj(x, w, tabs, seq_len):
    n = x.shape[0]
    tm = min(TILE["in_proj"], seq_len)
    per_seq = seq_len // tm
    tab_spec = pl.BlockSpec((3, tm, LANES), lambda i: (0, i % per_seq, 0))
    row = lambda width: pl.BlockSpec((tm, width), lambda i: (i, 0))
    widths = (KVQ_WIDTH, SSM_WIDTH, 3 * RET_WIDTH, RET_WIDTH)
    return pl.pallas_call(
        _in_proj_kernel,
        grid=(n // tm,),
        in_specs=[row(D_MODEL), _const_spec((D_MODEL, IN_WIDTH)), tab_spec, tab_spec, tab_spec, tab_spec],
        out_specs=[row(wd) for wd in widths],
        out_shape=[jax.ShapeDtypeStruct((n, wd), BF16) for wd in widths],
        compiler_params=_cparams("parallel"),
        name="in_proj",
    )(x, w, tabs["ak"], tabs["aq"], tabs["rq"], tabs["rk"])


def _attn_kernel(sink_ref, main_ref, prev_ref, next_ref, g_ref, o_ref, *, seq_len, tq):
    i = pl.program_id(1)
    w = ATTN_WINDOW
    kv_all = jnp.concatenate([prev_ref[...], main_ref[:, 0:2 * LANES], next_ref[...]], axis=0).astype(F32)
    low = (lax.broadcasted_iota(jnp.int32, kv_all.shape, 1) & (LANES - 1)) < HEAD_DIM
    zero = jnp.zeros_like(kv_all)
    per_kv = (jnp.where(low, kv_all, zero).astype(BF16), jnp.where(low, zero, kv_all).astype(BF16))

    qi = lax.broadcasted_iota(jnp.int32, (w, w), 0)
    kj = lax.broadcasted_iota(jnp.int32, (w, w), 1)
    g = g_ref[...]

    for sb in range(tq // w):
        r0 = sb * w
        pos0 = i * tq + r0
        left_ok = kj >= qi + jnp.where(pos0 >= w, 0, w)
        right_ok = kj <= qi - jnp.where(pos0 + w < seq_len, 0, w)
        q3 = jnp.concatenate([main_ref[r0:r0 + w, (2 + c) * LANES:(3 + c) * LANES] for c in range(3)], axis=0)
        acc = None
        for kvh in range(ATTN_KV_HEADS):
            keys = per_kv[kvh][r0:r0 + 3 * w, 0:LANES]
            vals = per_kv[kvh][r0:r0 + 3 * w, LANES:2 * LANES]
            s = _dot_nt(q3, keys)
            probs, invs = [], []
            for c in range(3):
                sk = sink_ref[kvh * 3 + c]
                sc = s[c * w:(c + 1) * w, :]
                s_l = jnp.where(left_ok, sc[:, 0:w], -1e30)
                s_m = sc[:, w:2 * w]
                s_r = jnp.where(right_ok, sc[:, 2 * w:], -1e30)
                m = jnp.maximum(jnp.max(jnp.maximum(jnp.maximum(s_l, s_m), s_r), axis=-1, keepdims=True), sk)
                p_l, p_m, p_r = jnp.exp(s_l - m), jnp.exp(s_m - m), jnp.exp(s_r - m)
                den = jnp.sum(p_l + p_m + p_r, axis=-1, keepdims=True) + jnp.exp(sk - m)
                invs.append(1.0 / den)
                probs.append(jnp.concatenate([p_l, p_m, p_r], axis=1).astype(BF16))
            pv = _dot(jnp.concatenate(probs, axis=0), vals) * jnp.concatenate(invs, axis=0)
            acc = pv if acc is None else acc + pv
        a = jnp.concatenate([acc[0:w], acc[w:2 * w], acc[2 * w:]], axis=1)
        a = a * lax.rsqrt(jnp.mean(a * a, axis=-1, keepdims=True) + EPS) * g
        o_ref[r0:r0 + w, :] = a.astype(BF16)


def _ret_kernel(fwd_ref, bwd_ref, decay_ref, gq_ref, gk_ref, gc_ref, bd_ref, rf_ref, rb_ref, sf_ref, sb_ref, *, tr):
    j = pl.program_id(1)

    @pl.when(j == 0)
    def _():
        sf_ref[...] = jnp.zeros_like(sf_ref)
        sb_ref[...] = jnp.zeros_like(sb_ref)

    nch = tr // RET_CHUNK
    half = lax.broadcasted_iota(jnp.int32, (RET_CHUNK, LANES), 1) < HEAD_DIM
    zero = jnp.zeros((RET_CHUNK, LANES), F32)
    bd = bd_ref[...]

    def parts(ref, r0, c):
        col = lambda base: ref[r0:r0 + RET_CHUNK, base + c * LANES:base + (c + 1) * LANES]
        return col(0), col(RET_WIDTH), col(2 * RET_WIDTH)

    for ci in range(nch):
        r0 = ci * RET_CHUNK
        for c in range(3):
            lanes = slice(c * LANES, (c + 1) * LANES)
            q, k, v = parts(fwd_ref, r0, c)
            kf, vf = k.astype(F32), v.astype(F32)
            k2 = jnp.concatenate([jnp.where(half, kf, zero), jnp.where(half, zero, kf)], axis=0).astype(BF16)
            v2 = jnp.concatenate([jnp.where(half, vf, zero), jnp.where(half, zero, vf)], axis=0).astype(BF16)
            s = _dot_nt(q, k2) * decay_ref[c]
            qf = (q.astype(F32) * gq_ref[0, :, lanes]).astype(BF16)
            lhs = jnp.concatenate([s.astype(BF16), qf], axis=1)
            rhs = jnp.concatenate([v2, sf_ref[c].astype(BF16)], axis=0)
            rf_ref[r0:r0 + RET_CHUNK, lanes] = _dot(lhs, rhs).astype(BF16)
            kd = (kf * gk_ref[0, :, lanes]).astype(BF16)
            sf_ref[c] = gc_ref[c] * sf_ref[c] + _dot_tn(kd, v) * bd
        r0 = (nch - 1 - ci) * RET_CHUNK
        for c in range(3):
            lanes = slice(c * LANES, (c + 1) * LANES)
            q, k, v = parts(bwd_ref, r0, c)
            qb = (q.astype(F32) * gq_ref[1, :, lanes]).astype(BF16)
            rb_ref[r0:r0 + RET_CHUNK, lanes] = _dot(qb, sb_ref[c].astype(BF16)).astype(BF16)
            kd = (k.astype(F32) * gk_ref[1, :, lanes]).astype(BF16)
            sb_ref[c] = gc_ref[c] * sb_ref[c] + _dot_tn(kd, v) * bd


def _retention(rqkv, consts, batch, seq_len):
    n = rqkv.shape[0]
    tr = min(TILE["retention"], seq_len)
    per_seq = seq_len // tr
    fwd_map = lambda b, j: (b * per_seq + j, 0)
    bwd_map = lambda b, j: (b * per_seq + per_seq - 1 - j, 0)
    return pl.pallas_call(
        functools.partial(_ret_kernel, tr=tr),
        grid=(batch, per_seq),
        in_specs=[pl.BlockSpec((tr, 3 * RET_WIDTH), fwd_map), pl.BlockSpec((tr, 3 * RET_WIDTH), bwd_map),
                  _const_spec((3, RET_CHUNK, 2 * RET_CHUNK)), _const_spec((2, RET_CHUNK, RET_WIDTH)),
                  _const_spec((2, RET_CHUNK, RET_WIDTH)), _const_spec((3, LANES, LANES)),
                  _const_spec((LANES, LANES))],
        out_specs=[pl.BlockSpec((tr, RET_WIDTH), fwd_map), pl.BlockSpec((tr, RET_WIDTH), bwd_map)],
        out_shape=[jax.ShapeDtypeStruct((n, RET_WIDTH), BF16)] * 2,
        scratch_shapes=[pltpu.VMEM((3, LANES, LANES), F32), pltpu.VMEM((3, LANES, LANES), F32)],
        compiler_params=_cparams("parallel", "arbitrary"),
        name="retention",
    )(rqkv, rqkv, consts["decay"], consts["gq"], consts["gk"], consts["gc"], consts["bd"])


def _cmul(ar, ai, br, bi):
    return ar * br - ai * bi, ar * bi + ai * br


S5T_BLOCK = 16
S5T_ROWS = 8
S5T_WIDTH = SSM_WIDTH * S5T_BLOCK
S5T_PAIR = 2 * SSM_GROUP * S5T_BLOCK


def _s5t_kernel(uf_ref, ub_ref, m_ref, bcf_ref, bcb_ref, ccf_ref, ccb_ref, tr_ref, ti_ref, yf_ref, yb_ref,
                sr_ref, si_ref, cr_ref, ci_ref, *, rows):
    @pl.when(pl.program_id(1) == 0)
    def _():
        cr_ref[...] = jnp.zeros_like(cr_ref)
        ci_ref[...] = jnp.zeros_like(ci_ref)

    npair = SSM_GROUPS // 2
    for z, (u_ref, b_ref) in enumerate(((uf_ref, bcf_ref), (ub_ref, bcb_ref))):
        for q in range(npair):
            v = _dot(u_ref[:, q * S5T_PAIR:(q + 1) * S5T_PAIR], b_ref[q])
            sr_ref[z, :, q * LANES:(q + 1) * LANES] = v[:, :LANES]
            si_ref[z, :, q * LANES:(q + 1) * LANES] = v[:, LANES:]

    ntile = rows // S5T_ROWS
    row = lax.broadcasted_iota(jnp.int32, (S5T_ROWS, LANES), 0)

    def scan_tile(i, carry):
        for z in range(2):
            t0 = pl.multiple_of((i if z == 0 else ntile - 1 - i) * S5T_ROWS, S5T_ROWS)
            edge = S5T_ROWS - 1 if z == 0 else 0
            for q in range(npair):
                lanes = slice(q * LANES, (q + 1) * LANES)
                xr = sr_ref[z, pl.ds(t0, S5T_ROWS), lanes]
                xi = si_ref[z, pl.ds(t0, S5T_ROWS), lanes]
                for k_idx, k in enumerate((1, 2, 4)):
                    sh = k if z == 0 else S5T_ROWS - k
                    rr, ri = pltpu.roll(xr, sh, 0), pltpu.roll(xi, sh, 0)
                    pr, pi = tr_ref[z, k_idx, :, lanes], ti_ref[z, k_idx, :, lanes]
                    xr, xi = xr + (pr * rr - pi * ri), xi + (pr * ri + pi * rr)
                c_r, c_i = cr_ref[z, :, lanes], ci_ref[z, :, lanes]
                ar, ai = tr_ref[z, 3, :, lanes], ti_ref[z, 3, :, lanes]
                xr, xi = xr + (ar * c_r - ai * c_i), xi + (ar * c_i + ai * c_r)
                cr_ref[z, :, lanes] = jnp.broadcast_to(xr[edge:edge + 1, :], (S5T_ROWS, LANES))
                ci_ref[z, :, lanes] = jnp.broadcast_to(xi[edge:edge + 1, :], (S5T_ROWS, LANES))
                back = 1 if z == 0 else S5T_ROWS - 1
                first = row == (0 if z == 0 else S5T_ROWS - 1)
                sr_ref[z, pl.ds(t0, S5T_ROWS), lanes] = jnp.where(first, c_r, pltpu.roll(xr, back, 0))
                si_ref[z, pl.ds(t0, S5T_ROWS), lanes] = jnp.where(first, c_i, pltpu.roll(xi, back, 0))
        return carry

    lax.fori_loop(0, ntile, scan_tile, 0)

    for q in range(npair):
        lanes = slice(q * LANES, (q + 1) * LANES)
        xf = jnp.concatenate([sr_ref[0, :, lanes], si_ref[0, :, lanes]], axis=1).astype(BF16)
        xb = jnp.concatenate([sr_ref[1, :, lanes], si_ref[1, :, lanes]], axis=1).astype(BF16)
        carry_f = _dot(xf, ccf_ref[q])
        for gl in range(2):
            g = 2 * q + gl
            cols = slice(g * MXU_DIM, (g + 1) * MXU_DIM)
            intra = _dot(uf_ref[:, cols], m_ref[g])
            yf_ref[:, cols] = (intra + carry_f[:, gl * MXU_DIM:(gl + 1) * MXU_DIM]).astype(BF16)
        yb_ref[:, q * S5T_PAIR:(q + 1) * S5T_PAIR] = _dot(xb, ccb_ref[q]).astype(BF16)


def _s5t(u, p, batch, seq_len):
    n = u.shape[0]
    nb = n // S5T_BLOCK
    to_blocks = lambda a: a.reshape(nb, S5T_BLOCK, SSM_GROUPS, SSM_GROUP).transpose(0, 2, 1, 3).reshape(nb, S5T_WIDTH)
    from_blocks = lambda a: a.reshape(nb, SSM_GROUPS, S5T_BLOCK, SSM_GROUP).transpose(0, 2, 1, 3).reshape(n, SSM_WIDTH)
    ub = to_blocks(u)
    rows = min(TILE["s5"], seq_len) // S5T_BLOCK
    per_seq = seq_len // S5T_BLOCK // rows
    fwd_map = lambda b, j: (b * per_seq + j, 0)
    bwd_map = lambda b, j: (b * per_seq + per_seq - 1 - j, 0)
    blk = lambda m: pl.BlockSpec((rows, S5T_WIDTH), m)
    npair = SSM_GROUPS // 2
    yf, yb = pl.pallas_call(
        functools.partial(_s5t_kernel, rows=rows),
        grid=(batch, per_seq),
        in_specs=[blk(fwd_map), blk(bwd_map), _const_spec((SSM_GROUPS, MXU_DIM, MXU_DIM)),
                  _const_spec((npair, S5T_PAIR, MXU_DIM)), _const_spec((npair, S5T_PAIR, MXU_DIM)),
                  _const_spec((npair, MXU_DIM, S5T_PAIR)), _const_spec((npair, MXU_DIM, S5T_PAIR)),
                  _const_spec((2, 4, S5T_ROWS, SSM_LANES)), _const_spec((2, 4, S5T_ROWS, SSM_LANES))],
        out_specs=[blk(fwd_map), blk(bwd_map)],
        out_shape=[jax.ShapeDtypeStruct((nb, S5T_WIDTH), BF16)] * 2,
        scratch_shapes=[pltpu.VMEM((2, rows, SSM_LANES), F32), pltpu.VMEM((2, rows, SSM_LANES), F32),
                        pltpu.VMEM((2, S5T_ROWS, SSM_LANES), F32), pltpu.VMEM((2, S5T_ROWS, SSM_LANES), F32)],
        compiler_params=_cparams("parallel", "arbitrary"),
        name="s5t",
    )(ub, ub, p["m"], p["bcf"], p["bcb"], p["ccf"], p["ccb"], p["tr"], p["ti"])
    return from_blocks(yf.astype(F32) + yb.astype(F32)).astype(BF16)


def _layer_norm(x, g, b):
    mu = jnp.mean(x, axis=-1, keepdims=True)
    xc = x - mu
    var = jnp.mean(xc * xc, axis=-1, keepdims=True)
    return xc * lax.rsqrt(var + EPS) * g + b


def _head_mean(x, avg_ref):
    xb = x.astype(BF16)
    return jnp.concatenate([_dot(xb[:, :MXU_DIM], avg_ref[...]),
                            _dot(xb[:, MXU_DIM:], avg_ref[0:LANES, 0:LANES])], axis=1)


def _sigmoid(x):
    return 1.0 / (1.0 + jnp.exp(-x))


def _mix_kernel(x_ref, a_ref, y_ref, rf_ref, rb_ref, rg_ref, gluw_ref, glub_ref, sg_ref, avg_ref,
                wout_ref, g_ref, b_ref, o_ref):
    y = y_ref[...].astype(F32)
    s = 0.5 * y * (1.0 + jnp.tanh(math.sqrt(2.0 / math.pi) * (y + 0.044715 * (y * y * y))))
    s = s * _sigmoid(_dot(s.astype(BF16), gluw_ref[...]) + glub_ref[...])
    s = s * lax.rsqrt(jnp.mean(s * s, axis=-1, keepdims=True) + EPS) * sg_ref[...]
    r = rf_ref[...].astype(F32) + rb_ref[...].astype(F32)
    rc = r - _head_mean(r, avg_ref)
    r = rc * lax.rsqrt(_head_mean(rc * rc, avg_ref) + EPS)
    rg = rg_ref[...].astype(F32)
    r = r * (rg * _sigmoid(rg))
    mix = (_dot(a_ref[...], wout_ref[0:ATTN_WIDTH, :])
           + _dot(s.astype(BF16), wout_ref[ATTN_WIDTH:ATTN_WIDTH + SSM_WIDTH, :])
           + _dot(r.astype(BF16), wout_ref[ATTN_WIDTH + SSM_WIDTH:, :]))
    o_ref[...] = _layer_norm(DN_ALPHA * x_ref[...] + mix, g_ref[...], b_ref[...])


def _attn_mix_kernel(sink_ref, main_ref, prev_ref, next_ref, ag_ref, x_ref, y_ref, rf_ref, rb_ref, rg_ref,
                     gluw_ref, glub_ref, sg_ref, avg_ref, wout_ref, g_ref, b_ref, o_ref, a_ref, *, seq_len, tq):
    _attn_kernel(sink_ref, main_ref, prev_ref, next_ref, ag_ref, a_ref, seq_len=seq_len, tq=tq)
    _mix_kernel(x_ref, a_ref, y_ref, rf_ref, rb_ref, rg_ref, gluw_ref, glub_ref, sg_ref, avg_ref,
                wout_ref, g_ref, b_ref, o_ref)


def _attn_mix(x, kvq, y, rf, rb, rg, lp, avg, batch, seq_len):
    n = x.shape[0]
    tq = min(TILE["attn_mix"], seq_len)
    per_seq = seq_len // tq
    sub = tq // ATTN_WINDOW
    last = n // ATTN_WINDOW - 1
    row = lambda width: pl.BlockSpec((tq, width), lambda b, i: (b * per_seq + i, 0))
    prev = pl.BlockSpec((ATTN_WINDOW, 2 * LANES), lambda b, i: (jnp.maximum((b * per_seq + i) * sub - 1, 0), 0))
    nxt = pl.BlockSpec((ATTN_WINDOW, 2 * LANES), lambda b, i: (jnp.minimum((b * per_seq + i + 1) * sub, last), 0))
    return pl.pallas_call(
        functools.partial(_attn_mix_kernel, seq_len=seq_len, tq=tq),
        grid=(batch, per_seq),
        in_specs=[pl.BlockSpec(memory_space=pltpu.SMEM), row(KVQ_WIDTH), prev, nxt, _const_spec((1, ATTN_WIDTH)),
                  row(D_MODEL), row(SSM_WIDTH), row(RET_WIDTH), row(RET_WIDTH), row(RET_WIDTH),
                  _const_spec((SSM_WIDTH, SSM_WIDTH)), _const_spec((1, SSM_WIDTH)), _const_spec((1, SSM_WIDTH)),
                  _const_spec((MXU_DIM, MXU_DIM)), _const_spec((MIX_WIDTH, D_MODEL)), _const_spec((1, D_MODEL)),
                  _const_spec((1, D_MODEL))],
        out_specs=row(D_MODEL),
        out_shape=jax.ShapeDtypeStruct((n, D_MODEL), F32),
        scratch_shapes=[pltpu.VMEM((tq, ATTN_WIDTH), BF16)],
        compiler_params=_cparams("parallel", "parallel"),
        name="attn_mix",
    )(lp["sink"], kvq, kvq, kvq, lp["attn_g"], x, y, rf, rb, rg, lp["glu_w"], lp["glu_b"], lp["ssm_g"], avg,
      lp["w_out"], lp["ln1_g"], lp["ln1_b"])


def _ffn_kernel(x_ref, prev_ref, next_ref, wup_ref, cw_ref, cb_ref, wdn_ref, g_ref, b_ref, o_ref, xe_ref,
                gated_ref, *, seq_len, tm):
    i = pl.program_id(0)
    per_seq = seq_len // tm
    first = (i % per_seq) == 0
    last = (i % per_seq) == per_seq - 1
    halo = BF16_ROWS
    xe_ref[0:halo, :] = jnp.where(first, 0.0, prev_ref[...]).astype(BF16)
    xe_ref[halo:halo + tm, :] = x_ref[...].astype(BF16)
    xe_ref[halo + tm:, :] = jnp.where(last, 0.0, next_ref[...]).astype(BF16)
    rows = lax.broadcasted_iota(jnp.int32, (tm, FF_CHUNK), 0)
    for c in range(D_FF // FF_CHUNK):
        cols = slice(c * FF_CHUNK, (c + 1) * FF_CHUNK)
        val = _dot(xe_ref[halo:halo + tm, :], wup_ref[:, cols])
        act_e = _dot(xe_ref[...], wup_ref[:, D_FF + c * FF_CHUNK:D_FF + (c + 1) * FF_CHUNK])
        act = act_e[halo:halo + tm, :]
        before = jnp.where(rows == 0, act_e[halo - 1:halo, :], pltpu.roll(act, 1, 0))
        after = jnp.where(rows == tm - 1, act_e[halo + tm:halo + tm + 1, :], pltpu.roll(act, tm - 1, 0))
        cv = before * cw_ref[0:1, cols] + act * cw_ref[1:2, cols] + after * cw_ref[2:3, cols] + cb_ref[:, cols]
        gated_ref[:, cols] = (cv * _sigmoid(cv) * val).astype(BF16)
    th = tm // FFN_ROW_SPLIT
    for r in range(FFN_ROW_SPLIT):
        rs = slice(r * th, (r + 1) * th)
        ffn = _dot(gated_ref[rs, :], wdn_ref[...])
        o_ref[rs, :] = _layer_norm(DN_ALPHA * x_ref[rs, :] + ffn, g_ref[...], b_ref[...])


def _ffn(x, lp, seq_len):
    n = x.shape[0]
    tm = min(TILE["ffn"], seq_len)
    halo = BF16_ROWS
    sub = tm // halo
    last = n // halo - 1
    return pl.pallas_call(
        functools.partial(_ffn_kernel, seq_len=seq_len, tm=tm),
        grid=(n // tm,),
        in_specs=[pl.BlockSpec((tm, D_MODEL), lambda i: (i, 0)),
                  pl.BlockSpec((halo, D_MODEL), lambda i: (jnp.maximum(i * sub - 1, 0), 0)),
                  pl.BlockSpec((halo, D_MODEL), lambda i: (jnp.minimum((i + 1) * sub, last), 0)),
                  _const_spec((D_MODEL, 2 * D_FF)), _const_spec((3, D_FF)), _const_spec((1, D_FF)),
                  _const_spec((D_FF, D_MODEL)), _const_spec((1, D_MODEL)), _const_spec((1, D_MODEL))],
        out_specs=pl.BlockSpec((tm, D_MODEL), lambda i: (i, 0)),
        out_shape=jax.ShapeDtypeStruct((n, D_MODEL), F32),
        scratch_shapes=[pltpu.VMEM((tm + 2 * halo, D_MODEL), BF16), pltpu.VMEM((tm, D_FF), BF16)],
        compiler_params=_cparams("parallel"),
        name="ffn",
    )(x, x, x, lp["w_up"], lp["conv_w"], lp["conv_b"], lp["w_down"], lp["ln2_g"], lp["ln2_b"])


def _rotary_tables(seq_len):
    pos = jnp.arange(seq_len, dtype=F32)[:, None]

    def build(inv_freq, scale):
        half = inv_freq.shape[0]
        ang = pos * inv_freq[None, :]
        cos, sin = jnp.cos(ang), jnp.sin(ang)
        pad = HEAD_DIM - 2 * half
        c = jnp.concatenate([cos, cos, jnp.ones((seq_len, pad), F32)], axis=1)
        s_up = jnp.concatenate([-sin, jnp.zeros((seq_len, half + pad), F32)], axis=1)
        s_dn = jnp.concatenate([jnp.zeros((seq_len, half), F32), sin, jnp.zeros((seq_len, pad), F32)], axis=1)
        t = jnp.stack([c, s_up, s_dn]) * scale
        return jnp.tile(t, (1, 1, LANES // HEAD_DIM))

    inv_a = ROPE_THETA ** (-jnp.arange(0, ROPE_DIM, 2, dtype=F32) / ROPE_DIM)
    inv_r = 1.0 / (RET_THETA ** jnp.linspace(0.0, 1.0, HEAD_DIM // 2, dtype=F32))
    scale = HEAD_DIM ** -0.5
    return {"ak": build(inv_a, 1.0), "aq": build(inv_a, scale), "rq": build(inv_r, 1.0), "rk": build(inv_r, scale)}


def _retention_consts():
    c = RET_CHUNK
    log_g = jnp.log(1.0 - 2.0 ** (-5.0 - jnp.arange(RET_HEADS, dtype=F32)))
    idx = jnp.arange(c, dtype=F32)
    decay = jnp.exp(log_g[:, None, None] * jnp.abs(idx[:, None] - idx[None, :]))
    decay = decay.reshape(3, 2, c, c).transpose(0, 2, 1, 3).reshape(3, c, 2 * c)
    per_lane = lambda e: jnp.repeat(jnp.exp(log_g[None, :] * e[:, None]), HEAD_DIM, axis=1)
    gq = jnp.stack([per_lane(idx + 1.0), per_lane(c - idx)])
    gk = jnp.stack([per_lane(c - 1.0 - idx), per_lane(idx)])
    g_chunk = jnp.repeat(jnp.exp(log_g * c), HEAD_DIM).reshape(3, LANES)
    gc = jnp.broadcast_to(g_chunk[:, :, None], (3, LANES, LANES))
    blk = jnp.arange(LANES) // HEAD_DIM
    bd = (blk[:, None] == blk[None, :]).astype(F32)
    seg = jnp.arange(MXU_DIM) // HEAD_DIM
    avg = ((seg[:, None] == seg[None, :]).astype(F32) / HEAD_DIM).astype(BF16)
    return {"decay": decay, "gq": gq, "gk": gk, "gc": gc, "bd": bd}, avg


def _powers(br, bi, count):
    pr, pi = jnp.ones_like(br)[None], jnp.zeros_like(bi)[None]
    sr, si = br, bi
    while pr.shape[0] < count:
        nr, ni = _cmul(pr, pi, sr[None], si[None])
        pr, pi = jnp.concatenate([pr, nr]), jnp.concatenate([pi, ni])
        sr, si = _cmul(sr, si, sr, si)
    return pr[:count], pi[:count]


def _s5t_params(lam_re, lam_im, log_dt, b_re, b_im, c_re, c_im, d_skip):
    tb, g_, p_, h_ = S5T_BLOCK, SSM_GROUPS, SSM_STATE, SSM_GROUP
    dt = jnp.exp(log_dt)[:, :, None]
    mag = jnp.exp(lam_re * dt)
    lbr, lbi = mag * jnp.cos(lam_im * dt), mag * jnp.sin(lam_im * dt)
    den = lam_re * lam_re + lam_im * lam_im
    cfr, cfi = _cmul(lbr - 1.0, lbi, lam_re / den, -lam_im / den)
    bbr, bbi = _cmul(cfr[..., None], cfi[..., None], b_re, b_im)
    pwr, pwi = _powers(lbr, lbi, tb + 1)
    clr, cli = _cmul(c_re[None], c_im[None], pwr[:, :, :, None, :], pwi[:, :, :, None, :])
    taps = (jnp.einsum('tzgap,zgph->tzgah', clr[:tb], bbr) - jnp.einsum('tzgap,zgph->tzgah', cli[:tb], bbi))
    t_in = jnp.arange(tb)[:, None]
    t_out = jnp.arange(tb)[None, :]
    lag = t_out - t_in
    kf = jnp.where((lag >= 0)[:, :, None, None, None], taps[jnp.clip(lag, 0, tb - 1), 0], 0.0)
    kb = jnp.where((lag <= 0)[:, :, None, None, None], taps[jnp.clip(-lag, 0, tb - 1), 1], 0.0)
    skip = (jnp.eye(tb)[:, :, None, None, None] * jnp.eye(h_)[None, None, None]
            * d_skip.reshape(1, 1, g_, h_, 1))
    m = (kf + kb + skip).transpose(2, 0, 4, 1, 3).reshape(g_, tb * h_, tb * h_)

    eye2 = jnp.eye(2, dtype=F32)

    def inject(wr, wi):
        def blk(w):
            w = w.reshape(tb, g_ // 2, 2, p_, h_)
            return jnp.einsum('tqgph,gk->qgthkp', w, eye2).reshape(g_ // 2, 2 * tb * h_, 2 * p_)
        return jnp.concatenate([blk(wr), blk(wi)], axis=2)

    def readout(cr, ci):
        def blk(w):
            w = w.reshape(tb, g_ // 2, 2, h_, p_)
            return jnp.einsum('tqghp,gk->qgpkth', w, eye2).reshape(g_ // 2, 2 * p_, 2 * tb * h_)
        return jnp.concatenate([blk(cr), -blk(ci)], axis=1)

    wfr, wfi = _cmul(pwr[tb - 1::-1, 0][:tb, :, :, None], pwi[tb - 1::-1, 0][:tb, :, :, None], bbr[0][None], bbi[0][None])
    wbr, wbi = _cmul(pwr[:tb, 1][:, :, :, None], pwi[:tb, 1][:, :, :, None], bbr[1][None], bbi[1][None])
    bcf, bcb = inject(wfr, wfi), inject(wbr, wbi)
    ccf = readout(clr[1:tb + 1, 0], cli[1:tb + 1, 0])
    ccb = readout(clr[tb:0:-1, 1], cli[tb:0:-1, 1])

    flat = lambda a: a.reshape(2, SSM_LANES)
    ar, ai = _powers(flat(pwr[tb]), flat(pwi[tb]), S5T_ROWS + 1)
    r = jnp.arange(S5T_ROWS)

    def tables(part):
        out = []
        for z in range(2):
            kinds = []
            for k in (1, 2, 4):
                ok = (r >= k) if z == 0 else (r <= S5T_ROWS - 1 - k)
                kinds.append(jnp.where(ok[:, None], part[k, z][None, :], 0.0))
            kinds.append(part[r + 1, z] if z == 0 else part[S5T_ROWS - r, z])
            out.append(jnp.stack(kinds))
        return jnp.stack(out)

    return {"m": m.astype(BF16), "bcf": bcf.astype(BF16), "bcb": bcb.astype(BF16), "ccf": ccf.astype(BF16),
            "ccb": ccb.astype(BF16), "tr": tables(ar), "ti": tables(ai)}


def _trunk(x3, layers, ret_consts, avg):
    batch, seq_len, _ = x3.shape
    x = x3.reshape(batch * seq_len, D_MODEL)
    tabs = _rotary_tables(seq_len)
    for lp in layers:
        kvq, u, rqkv, rg = _in_proj(x, lp["w_in"], tabs, seq_len)
        rf, rb = _retention(rqkv, ret_consts, batch, seq_len)
        y = _s5t(u, lp["s5"], batch, seq_len)
        x = _attn_mix(x, kvq, y, rf, rb, rg, lp, avg, batch, seq_len)
        x = _ffn(x, lp, seq_len)
    return x.reshape(batch, seq_len, D_MODEL)


def _prepare_layers(w_in, attn_sink, attn_out_g, lam_re, lam_im, log_dt, b_re, b_im, c_re, c_im, ssm_d, glu_w,
                    glu_b, ssm_out_g, w_out, ln1_g, ln1_b, w_up, conv_w, conv_b, w_down, ln2_g, ln2_b):
    depth = w_in.shape[0]
    heads = [h for c in range(3) for h in (c, 3 + c)]
    regroup = lambda a, axis: [lax.slice_in_dim(a, h * HEAD_DIM, (h + 1) * HEAD_DIM, axis=axis) for h in heads]
    w_in = jnp.concatenate([w_in[:, :, 384:640]] + regroup(w_in, 2) + [w_in[:, :, 640:]], axis=2).astype(BF16)
    w_out = jnp.concatenate(regroup(w_out, 1) + [w_out[:, ATTN_WIDTH:, :]], axis=1).astype(BF16)
    attn_out_g = jnp.concatenate(regroup(attn_out_g, 1), axis=1)
    layers = []
    for l in range(depth):
        layers.append({
            "w_in": w_in[l], "sink": attn_sink[l], "attn_g": attn_out_g[l].reshape(1, ATTN_WIDTH),
            "s5": _s5t_params(lam_re[l], lam_im[l], log_dt[l], b_re[l], b_im[l], c_re[l], c_im[l], ssm_d[l]),
            "glu_w": glu_w[l].astype(BF16), "glu_b": glu_b[l].reshape(1, SSM_WIDTH),
            "ssm_g": ssm_out_g[l].reshape(1, SSM_WIDTH), "w_out": w_out[l],
            "ln1_g": ln1_g[l].reshape(1, D_MODEL), "ln1_b": ln1_b[l].reshape(1, D_MODEL),
            "w_up": w_up[l].astype(BF16), "conv_w": conv_w[l], "conv_b": conv_b[l].reshape(1, D_FF),
            "w_down": w_down[l].astype(BF16),
            "ln2_g": ln2_g[l].reshape(1, D_MODEL), "ln2_b": ln2_b[l].reshape(1, D_MODEL),
        })
    return layers


def kernel(x_prompt, x_sample, w_in, attn_sink, attn_out_g, ssm_lambda_re, ssm_lambda_im, ssm_log_dt, ssm_b_re, ssm_b_im, ssm_c_re, ssm_c_im, ssm_d, ssm_glu_w, ssm_glu_b, ssm_out_g, w_out, ln1_g, ln1_b, ffn_w_up, ffn_conv_w, ffn_conv_b, ffn_w_down, ln2_g, ln2_b):
    layers = _prepare_layers(w_in, attn_sink, attn_out_g, ssm_lambda_re, ssm_lambda_im, ssm_log_dt, ssm_b_re,
                             ssm_b_im, ssm_c_re, ssm_c_im, ssm_d, ssm_glu_w, ssm_glu_b, ssm_out_g, w_out, ln1_g,
                             ln1_b, ffn_w_up, ffn_conv_w, ffn_conv_b, ffn_w_down, ln2_g, ln2_b)
    ret_consts, avg = _retention_consts()
    return (_trunk(x_prompt, layers, ret_consts, avg), _trunk(x_sample, layers, ret_consts, avg))
```

```python
import functools
import math

import jax
import jax.numpy as jnp
from jax import lax
from jax.experimental import pallas as pl
from jax.experimental.pallas import tpu as pltpu

F32 = jnp.float32
BF16 = jnp.bfloat16

D_MODEL = 1024
DEPTH = 4
HEAD_DIM = 64
ATTN_WIDTH = 384
ATTN_HEADS = 6
ATTN_KV_HEADS = 2
ATTN_WINDOW = 128
ROPE_THETA = 500000.0
ROPE_DIM = 16
SSM_WIDTH = 256
SSM_GROUP = 16
SSM_GROUPS = 16
SSM_STATE = 64
SSM_LANES = SSM_GROUPS * SSM_STATE
RET_WIDTH = 384
RET_HEADS = 6
RET_CHUNK = 128
RET_THETA = 10000.0
MIX_WIDTH = 1024
IN_WIDTH = 2432
D_FF = 2816
DN_ALPHA = (2 * DEPTH) ** 0.25
EPS = 1e-5

LANES = 128
MXU_DIM = 256
BF16_ROWS = 16
VMEM_LIMIT = 56 * 1024 * 1024

KVQ_WIDTH = 2 * LANES + ATTN_WIDTH
S5_CHUNK = 128
S5_SLAB = 2 * LANES
S5_SLABS = 2 * SSM_LANES // S5_SLAB
FF_CHUNK = 256
FFN_ROW_SPLIT = 2

TILE = {"in_proj": 1024, "attn_mix": 512, "retention": 2048, "s5": 1024, "ffn": 512}

IN_COLUMNS = ([("k", 0), ("v", 0)] + [("q", c) for c in range(3)] + [("u", c) for c in range(2)]
              + [("rq", c) for c in range(3)] + [("rk", c) for c in range(3)] + [("rv", c) for c in range(3)]
              + [("rg", c) for c in range(3)])


def _cparams(*sem):
    return pltpu.CompilerParams(dimension_semantics=sem, vmem_limit_bytes=VMEM_LIMIT)


def _const_spec(shape):
    nd = len(shape)
    return pl.BlockSpec(shape, lambda *_: (0,) * nd)


def _dot(a, b):
    return jnp.dot(a, b, preferred_element_type=F32)


def _dot_nt(a, b):
    return lax.dot_general(a, b, (((1,), (1,)), ((), ())), preferred_element_type=F32)


def _dot_tn(a, b):
    return lax.dot_general(a, b, (((0,), (0,)), ((), ())), preferred_element_type=F32)


def _rotate(xc, tab_ref, shift):
    up = pltpu.roll(xc, LANES - shift, 1)
    dn = pltpu.roll(xc, shift, 1)
    return xc * tab_ref[0] + up * tab_ref[1] + dn * tab_ref[2]


def _in_proj_kernel(x_ref, w_ref, ak_ref, aq_ref, rq_ref, rk_ref, kvq_ref, u_ref, rqkv_ref, rg_ref):
    xb = x_ref[...].astype(BF16)
    roles = {
        "k": (kvq_ref, 0, ak_ref, ROPE_DIM // 2), "v": (kvq_ref, LANES, None, 0),
        "q": (kvq_ref, 2 * LANES, aq_ref, ROPE_DIM // 2), "u": (u_ref, 0, None, 0),
        "rq": (rqkv_ref, 0, rq_ref, HEAD_DIM // 2), "rk": (rqkv_ref, RET_WIDTH, rk_ref, HEAD_DIM // 2),
        "rv": (rqkv_ref, 2 * RET_WIDTH, None, 0), "rg": (rg_ref, 0, None, 0),
    }
    for c0 in range(0, IN_WIDTH, MXU_DIM):
        width = min(MXU_DIM, IN_WIDTH - c0)
        h = _dot(xb, w_ref[:, c0:c0 + width])
        for j in range(width // LANES):
            role, col = IN_COLUMNS[c0 // LANES + j]
            dst, base, tab, shift = roles[role]
            hc = h[:, j * LANES:(j + 1) * LANES]
            if tab is not None:
                hc = _rotate(hc, tab, shift)
            dst[:, base + col * LANES:base + (col + 1) * LANES] = hc.astype(BF16)


def _in_proj(x, w, tabs, seq_len):
    n = x.shape[0]
    tm = min(TILE["in_proj"], seq_len)
    per_seq = seq_len // tm
    tab_spec = pl.BlockSpec((3, tm, LANES), lambda i: (0, i % per_seq, 0))
    row = lambda width: pl.BlockSpec((tm, width), lambda i: (i, 0))
    widths = (KVQ_WIDTH, SSM_WIDTH, 3 * RET_WIDTH, RET_WIDTH)
    return pl.pallas_call(
        _in_proj_kernel,
        grid=(n // tm,),
        in_specs=[row(D_MODEL), _const_spec((D_MODEL, IN_WIDTH)), tab_spec, tab_spec, tab_spec, tab_spec],
        out_specs=[row(wd) for wd in widths],
        out_shape=[jax.ShapeDtypeStruct((n, wd), BF16) for wd in widths],
        compiler_params=_cparams("parallel"),
        name="in_proj",
    )(x, w, tabs["ak"], tabs["aq"], tabs["rq"], tabs["rk"])


def _attn_kernel(sink_ref, main_ref, prev_ref, next_ref, g_ref, o_ref, *, seq_len, tq):
    i = pl.program_id(1)
    w = ATTN_WINDOW
    kv_all = jnp.concatenate([prev_ref[...], main_ref[:, 0:2 * LANES], next_ref[...]], axis=0).astype(F32)
    low = (lax.broadcasted_iota(jnp.int32, kv_all.shape, 1) & (LANES - 1)) < HEAD_DIM
    zero = jnp.zeros_like(kv_all)
    per_kv = (jnp.where(low, kv_all, zero).astype(BF16), jnp.where(low, zero, kv_all).astype(BF16))

    qi = lax.broadcasted_iota(jnp.int32, (w, w), 0)
    kj = lax.broadcasted_iota(jnp.int32, (w, w), 1)
    g = g_ref[...]

    for sb in range(tq // w):
        r0 = sb * w
        pos0 = i * tq + r0
        left_ok = kj >= qi + jnp.where(pos0 >= w, 0, w)
        right_ok = kj <= qi - jnp.where(pos0 + w < seq_len, 0, w)
        q3 = jnp.concatenate([main_ref[r0:r0 + w, (2 + c) * LANES:(3 + c) * LANES] for c in range(3)], axis=0)
        acc = None
        for kvh in range(ATTN_KV_HEADS):
            keys = per_kv[kvh][r0:r0 + 3 * w, 0:LANES]
            vals = per_kv[kvh][r0:r0 + 3 * w, LANES:2 * LANES]
            s = _dot_nt(q3, keys)
            probs, invs = [], []
            for c in range(3):
                sk = sink_ref[kvh * 3 + c]
                sc = s[c * w:(c + 1) * w, :]
                s_l = jnp.where(left_ok, sc[:, 0:w], -1e30)
                s_m = sc[:, w:2 * w]
                s_r = jnp.where(right_ok, sc[:, 2 * w:], -1e30)
                m = jnp.maximum(jnp.max(jnp.maximum(jnp.maximum(s_l, s_m), s_r), axis=-1, keepdims=True), sk)
                p_l, p_m, p_r = jnp.exp(s_l - m), jnp.exp(s_m - m), jnp.exp(s_r - m)
                den = jnp.sum(p_l + p_m + p_r, axis=-1, keepdims=True) + jnp.exp(sk - m)
                invs.append(1.0 / den)
                probs.append(jnp.concatenate([p_l, p_m, p_r], axis=1).astype(BF16))
            pv = _dot(jnp.concatenate(probs, axis=0), vals) * jnp.concatenate(invs, axis=0)
            acc = pv if acc is None else acc + pv
        a = jnp.concatenate([acc[0:w], acc[w:2 * w], acc[2 * w:]], axis=1)
        a = a * lax.rsqrt(jnp.mean(a * a, axis=-1, keepdims=True) + EPS) * g
        o_ref[r0:r0 + w, :] = a.astype(BF16)


def _ret_kernel(fwd_ref, bwd_ref, decay_ref, gq_ref, gk_ref, gc_ref, bd_ref, rf_ref, rb_ref, sf_ref, sb_ref, *, tr):
    j = pl.program_id(1)

    @pl.when(j == 0)
    def _():
        sf_ref[...] = jnp.zeros_like(sf_ref)
        sb_ref[...] = jnp.zeros_like(sb_ref)

    nch = tr // RET_CHUNK
    half = lax.broadcasted_iota(jnp.int32, (RET_CHUNK, LANES), 1) < HEAD_DIM
    zero = jnp.zeros((RET_CHUNK, LANES), F32)
    bd = bd_ref[...]

    def parts(ref, r0, c):
        col = lambda base: ref[r0:r0 + RET_CHUNK, base + c * LANES:base + (c + 1) * LANES]
        return col(0), col(RET_WIDTH), col(2 * RET_WIDTH)

    for ci in range(nch):
        r0 = ci * RET_CHUNK
        for c in range(3):
            lanes = slice(c * LANES, (c + 1) * LANES)
            q, k, v = parts(fwd_ref, r0, c)
            kf, vf = k.astype(F32), v.astype(F32)
            k2 = jnp.concatenate([jnp.where(half, kf, zero), jnp.where(half, zero, kf)], axis=0).astype(BF16)
            v2 = jnp.concatenate([jnp.where(half, vf, zero), jnp.where(half, zero, vf)], axis=0).astype(BF16)
            s = _dot_nt(q, k2) * decay_ref[c]
            qf = (q.astype(F32) * gq_ref[0, :, lanes]).astype(BF16)
            lhs = jnp.concatenate([s.astype(BF16), qf], axis=1)
            rhs = jnp.concatenate([v2, sf_ref[c].astype(BF16)], axis=0)
            rf_ref[r0:r0 + RET_CHUNK, lanes] = _dot(lhs, rhs).astype(BF16)
            kd = (kf * gk_ref[0, :, lanes]).astype(BF16)
            sf_ref[c] = gc_ref[c] * sf_ref[c] + _dot_tn(kd, v) * bd
        r0 = (nch - 1 - ci) * RET_CHUNK
        for c in range(3):
            lanes = slice(c * LANES, (c + 1) * LANES)
            q, k, v = parts(bwd_ref, r0, c)
            qb = (q.astype(F32) * gq_ref[1, :, lanes]).astype(BF16)
            rb_ref[r0:r0 + RET_CHUNK, lanes] = _dot(qb, sb_ref[c].astype(BF16)).astype(BF16)
            kd = (k.astype(F32) * gk_ref[1, :, lanes]).astype(BF16)
            sb_ref[c] = gc_ref[c] * sb_ref[c] + _dot_tn(kd, v) * bd


def _retention(rqkv, consts, batch, seq_len):
    n = rqkv.shape[0]
    tr = min(TILE["retention"], seq_len)
    per_seq = seq_len // tr
    fwd_map = lambda b, j: (b * per_seq + j, 0)
    bwd_map = lambda b, j: (b * per_seq + per_seq - 1 - j, 0)
    return pl.pallas_call(
        functools.partial(_ret_kernel, tr=tr),
        grid=(batch, per_seq),
        in_specs=[pl.BlockSpec((tr, 3 * RET_WIDTH), fwd_map), pl.BlockSpec((tr, 3 * RET_WIDTH), bwd_map),
                  _const_spec((3, RET_CHUNK, 2 * RET_CHUNK)), _const_spec((2, RET_CHUNK, RET_WIDTH)),
                  _const_spec((2, RET_CHUNK, RET_WIDTH)), _const_spec((3, LANES, LANES)),
                  _const_spec((LANES, LANES))],
        out_specs=[pl.BlockSpec((tr, RET_WIDTH), fwd_map), pl.BlockSpec((tr, RET_WIDTH), bwd_map)],
        out_shape=[jax.ShapeDtypeStruct((n, RET_WIDTH), BF16)] * 2,
        scratch_shapes=[pltpu.VMEM((3, LANES, LANES), F32), pltpu.VMEM((3, LANES, LANES), F32)],
        compiler_params=_cparams("parallel", "arbitrary"),
        name="retention",
    )(rqkv, rqkv, consts["decay"], consts["gq"], consts["gk"], consts["gc"], consts["bd"])


def _cmul(ar, ai, br, bi):
    return ar * br - ai * bi, ar * bi + ai * br


def _s5_kernel(uf_ref, ub_ref, bmat_ref, cmat_ref, linv_ref, lpow_ref, lam_ref, tri_ref, d_ref,
               yf_ref, yb_ref, st_ref, x_ref, *, ts):
    j = pl.program_id(1)

    @pl.when(j == 0)
    def _():
        st_ref[...] = jnp.zeros_like(st_ref)

    nch = ts // S5_CHUNK
    for z, (u_ref, y_ref) in enumerate(((uf_ref, yf_ref), (ub_ref, yb_ref))):
        edge = S5_CHUNK - 1 if z == 0 else 0
        u = u_ref[...]
        for s in range(S5_SLABS):
            re = slice(s * S5_SLAB, s * S5_SLAB + LANES)
            im = slice(s * S5_SLAB + LANES, (s + 1) * S5_SLAB)
            both = slice(s * S5_SLAB, (s + 1) * S5_SLAB)
            bu = _dot(u, bmat_ref[z, :, both])
            sr, si = st_ref[z, 0:1, re], st_ref[z, 0:1, im]
            for ci in range(nch):
                r0 = (ci if z == 0 else nch - 1 - ci) * S5_CHUNK
                rows = slice(r0, r0 + S5_CHUNK)
                zr, zi = _cmul(linv_ref[z, :, re], linv_ref[z, :, im], bu[rows, :LANES], bu[rows, LANES:])
                w = _dot(tri_ref[z], jnp.concatenate([zr, zi], axis=1).astype(BF16))
                cr, ci_ = _cmul(lam_ref[z, :, re], lam_ref[z, :, im], sr, si)
                xr, xi = _cmul(lpow_ref[z, :, re], lpow_ref[z, :, im], w[:, :LANES] + cr, w[:, LANES:] + ci_)
                sr, si = xr[edge:edge + 1, :], xi[edge:edge + 1, :]
                x_ref[z, rows, both] = jnp.concatenate([xr, xi], axis=1).astype(BF16)
            st_ref[z, 0:1, re] = sr
            st_ref[z, 0:1, im] = si
        y = _dot(x_ref[z], cmat_ref[z])
        if z == 0:
            y = y + d_ref[...] * u.astype(F32)
        y_ref[...] = y.astype(BF16)


def _s5(u, p, batch, seq_len):
    n = u.shape[0]
    ts = min(TILE["s5"], seq_len)
    per_seq = seq_len // ts
    fwd_map = lambda b, j: (b * per_seq + j, 0)
    bwd_map = lambda b, j: (b * per_seq + per_seq - 1 - j, 0)
    blk = lambda m: pl.BlockSpec((ts, SSM_WIDTH), m)
    return pl.pallas_call(
        functools.partial(_s5_kernel, ts=ts),
        grid=(batch, per_seq),
        in_specs=[blk(fwd_map), blk(bwd_map),
                  _const_spec((2, SSM_WIDTH, 2 * SSM_LANES)), _const_spec((2, 2 * SSM_LANES, SSM_WIDTH)),
                  _const_spec((2, S5_CHUNK, 2 * SSM_LANES)), _const_spec((2, S5_CHUNK, 2 * SSM_LANES)),
                  _const_spec((2, 1, 2 * SSM_LANES)), _const_spec((2, S5_CHUNK, S5_CHUNK)),
                  _const_spec((1, SSM_WIDTH))],
        out_specs=[blk(fwd_map), blk(bwd_map)],
        out_shape=[jax.ShapeDtypeStruct((n, SSM_WIDTH), BF16)] * 2,
        scratch_shapes=[pltpu.VMEM((2, 8, 2 * SSM_LANES), F32), pltpu.VMEM((2, ts, 2 * SSM_LANES), BF16)],
        compiler_params=_cparams("parallel", "arbitrary"),
        name="s5",
    )(u, u, p["bmat"], p["cmat"], p["linv"], p["lpow"], p["lam"], p["tri"], p["d"])


def _layer_norm(x, g, b):
    mu = jnp.mean(x, axis=-1, keepdims=True)
    xc = x - mu
    var = jnp.mean(xc * xc, axis=-1, keepdims=True)
    return xc * lax.rsqrt(var + EPS) * g + b


def _head_mean(x, avg_ref):
    xb = x.astype(BF16)
    return jnp.concatenate([_dot(xb[:, :MXU_DIM], avg_ref[...]),
                            _dot(xb[:, MXU_DIM:], avg_ref[0:LANES, 0:LANES])], axis=1)


def _sigmoid(x):
    return 1.0 / (1.0 + jnp.exp(-x))


def _mix_kernel(x_ref, a_ref, yf_ref, yb_ref, rf_ref, rb_ref, rg_ref, gluw_ref, glub_ref, sg_ref, avg_ref,
                wout_ref, g_ref, b_ref, o_ref):
    y = yf_ref[...].astype(F32) + yb_ref[...].astype(F32)
    s = 0.5 * y * (1.0 + jnp.tanh(math.sqrt(2.0 / math.pi) * (y + 0.044715 * (y * y * y))))
    s = s * _sigmoid(_dot(s.astype(BF16), gluw_ref[...]) + glub_ref[...])
    s = s * lax.rsqrt(jnp.mean(s * s, axis=-1, keepdims=True) + EPS) * sg_ref[...]
    r = rf_ref[...].astype(F32) + rb_ref[...].astype(F32)
    rc = r - _head_mean(r, avg_ref)
    r = rc * lax.rsqrt(_head_mean(rc * rc, avg_ref) + EPS)
    rg = rg_ref[...].astype(F32)
    r = r * (rg * _sigmoid(rg))
    mix = (_dot(a_ref[...], wout_ref[0:ATTN_WIDTH, :])
           + _dot(s.astype(BF16), wout_ref[ATTN_WIDTH:ATTN_WIDTH + SSM_WIDTH, :])
           + _dot(r.astype(BF16), wout_ref[ATTN_WIDTH + SSM_WIDTH:, :]))
    o_ref[...] = _layer_norm(DN_ALPHA * x_ref[...] + mix, g_ref[...], b_ref[...])


def _attn_mix_kernel(sink_ref, main_ref, prev_ref, next_ref, ag_ref, x_ref, yf_ref, yb_ref, rf_ref, rb_ref, rg_ref,
                     gluw_ref, glub_ref, sg_ref, avg_ref, wout_ref, g_ref, b_ref, o_ref, a_ref, *, seq_len, tq):
    _attn_kernel(sink_ref, main_ref, prev_ref, next_ref, ag_ref, a_ref, seq_len=seq_len, tq=tq)
    _mix_kernel(x_ref, a_ref, yf_ref, yb_ref, rf_ref, rb_ref, rg_ref, gluw_ref, glub_ref, sg_ref, avg_ref,
                wout_ref, g_ref, b_ref, o_ref)


def _attn_mix(x, kvq, yf, yb, rf, rb, rg, lp, avg, batch, seq_len):
    n = x.shape[0]
    tq = min(TILE["attn_mix"], seq_len)
    per_seq = seq_len // tq
    sub = tq // ATTN_WINDOW
    last = n // ATTN_WINDOW - 1
    row = lambda width: pl.BlockSpec((tq, width), lambda b, i: (b * per_seq + i, 0))
    prev = pl.BlockSpec((ATTN_WINDOW, 2 * LANES), lambda b, i: (jnp.maximum((b * per_seq + i) * sub - 1, 0), 0))
    nxt = pl.BlockSpec((ATTN_WINDOW, 2 * LANES), lambda b, i: (jnp.minimum((b * per_seq + i + 1) * sub, last), 0))
    return pl.pallas_call(
        functools.partial(_attn_mix_kernel, seq_len=seq_len, tq=tq),
        grid=(batch, per_seq),
        in_specs=[pl.BlockSpec(memory_space=pltpu.SMEM), row(KVQ_WIDTH), prev, nxt, _const_spec((1, ATTN_WIDTH)),
                  row(D_MODEL), row(SSM_WIDTH), row(SSM_WIDTH), row(RET_WIDTH), row(RET_WIDTH), row(RET_WIDTH),
                  _const_spec((SSM_WIDTH, SSM_WIDTH)), _const_spec((1, SSM_WIDTH)), _const_spec((1, SSM_WIDTH)),
                  _const_spec((MXU_DIM, MXU_DIM)), _const_spec((MIX_WIDTH, D_MODEL)), _const_spec((1, D_MODEL)),
                  _const_spec((1, D_MODEL))],
        out_specs=row(D_MODEL),
        out_shape=jax.ShapeDtypeStruct((n, D_MODEL), F32),
        scratch_shapes=[pltpu.VMEM((tq, ATTN_WIDTH), BF16)],
        compiler_params=_cparams("parallel", "parallel"),
        name="attn_mix",
    )(lp["sink"], kvq, kvq, kvq, lp["attn_g"], x, yf, yb, rf, rb, rg, lp["glu_w"], lp["glu_b"], lp["ssm_g"], avg,
      lp["w_out"], lp["ln1_g"], lp["ln1_b"])


def _ffn_kernel(x_ref, prev_ref, next_ref, wup_ref, cw_ref, cb_ref, wdn_ref, g_ref, b_ref, o_ref, xe_ref,
                gated_ref, *, seq_len, tm):
    i = pl.program_id(0)
    per_seq = seq_len // tm
    first = (i % per_seq) == 0
    last = (i % per_seq) == per_seq - 1
    halo = BF16_ROWS
    xe_ref[0:halo, :] = jnp.where(first, 0.0, prev_ref[...]).astype(BF16)
    xe_ref[halo:halo + tm, :] = x_ref[...].astype(BF16)
    xe_ref[halo + tm:, :] = jnp.where(last, 0.0, next_ref[...]).astype(BF16)
    rows = lax.broadcasted_iota(jnp.int32, (tm, FF_CHUNK), 0)
    for c in range(D_FF // FF_CHUNK):
        cols = slice(c * FF_CHUNK, (c + 1) * FF_CHUNK)
        val = _dot(xe_ref[halo:halo + tm, :], wup_ref[:, cols])
        act_e = _dot(xe_ref[...], wup_ref[:, D_FF + c * FF_CHUNK:D_FF + (c + 1) * FF_CHUNK])
        act = act_e[halo:halo + tm, :]
        before = jnp.where(rows == 0, act_e[halo - 1:halo, :], pltpu.roll(act, 1, 0))
        after = jnp.where(rows == tm - 1, act_e[halo + tm:halo + tm + 1, :], pltpu.roll(act, tm - 1, 0))
        cv = before * cw_ref[0:1, cols] + act * cw_ref[1:2, cols] + after * cw_ref[2:3, cols] + cb_ref[:, cols]
        gated_ref[:, cols] = (cv * _sigmoid(cv) * val).astype(BF16)
    th = tm // FFN_ROW_SPLIT
    for r in range(FFN_ROW_SPLIT):
        rs = slice(r * th, (r + 1) * th)
        ffn = _dot(gated_ref[rs, :], wdn_ref[...])
        o_ref[rs, :] = _layer_norm(DN_ALPHA * x_ref[rs, :] + ffn, g_ref[...], b_ref[...])


def _ffn(x, lp, seq_len):
    n = x.shape[0]
    tm = min(TILE["ffn"], seq_len)
    halo = BF16_ROWS
    sub = tm // halo
    last = n // halo - 1
    return pl.pallas_call(
        functools.partial(_ffn_kernel, seq_len=seq_len, tm=tm),
        grid=(n // tm,),
        in_specs=[pl.BlockSpec((tm, D_MODEL), lambda i: (i, 0)),
                  pl.BlockSpec((halo, D_MODEL), lambda i: (jnp.maximum(i * sub - 1, 0), 0)),
                  pl.BlockSpec((halo, D_MODEL), lambda i: (jnp.minimum((i + 1) * sub, last), 0)),
                  _const_spec((D_MODEL, 2 * D_FF)), _const_spec((3, D_FF)), _const_spec((1, D_FF)),
                  _const_spec((D_FF, D_MODEL)), _const_spec((1, D_MODEL)), _const_spec((1, D_MODEL))],
        out_specs=pl.BlockSpec((tm, D_MODEL), lambda i: (i, 0)),
        out_shape=jax.ShapeDtypeStruct((n, D_MODEL), F32),
        scratch_shapes=[pltpu.VMEM((tm + 2 * halo, D_MODEL), BF16), pltpu.VMEM((tm, D_FF), BF16)],
        compiler_params=_cparams("parallel"),
        name="ffn",
    )(x, x, x, lp["w_up"], lp["conv_w"], lp["conv_b"], lp["w_down"], lp["ln2_g"], lp["ln2_b"])


def _rotary_tables(seq_len):
    pos = jnp.arange(seq_len, dtype=F32)[:, None]

    def build(inv_freq, scale):
        half = inv_freq.shape[0]
        ang = pos * inv_freq[None, :]
        cos, sin = jnp.cos(ang), jnp.sin(ang)
        pad = HEAD_DIM - 2 * half
        c = jnp.concatenate([cos, cos, jnp.ones((seq_len, pad), F32)], axis=1)
        s_up = jnp.concatenate([-sin, jnp.zeros((seq_len, half + pad), F32)], axis=1)
        s_dn = jnp.concatenate([jnp.zeros((seq_len, half), F32), sin, jnp.zeros((seq_len, pad), F32)], axis=1)
        t = jnp.stack([c, s_up, s_dn]) * scale
        return jnp.tile(t, (1, 1, LANES // HEAD_DIM))

    inv_a = ROPE_THETA ** (-jnp.arange(0, ROPE_DIM, 2, dtype=F32) / ROPE_DIM)
    inv_r = 1.0 / (RET_THETA ** jnp.linspace(0.0, 1.0, HEAD_DIM // 2, dtype=F32))
    scale = HEAD_DIM ** -0.5
    return {"ak": build(inv_a, 1.0), "aq": build(inv_a, scale), "rq": build(inv_r, 1.0), "rk": build(inv_r, scale)}


def _retention_consts():
    c = RET_CHUNK
    log_g = jnp.log(1.0 - 2.0 ** (-5.0 - jnp.arange(RET_HEADS, dtype=F32)))
    idx = jnp.arange(c, dtype=F32)
    decay = jnp.exp(log_g[:, None, None] * jnp.abs(idx[:, None] - idx[None, :]))
    decay = decay.reshape(3, 2, c, c).transpose(0, 2, 1, 3).reshape(3, c, 2 * c)
    per_lane = lambda e: jnp.repeat(jnp.exp(log_g[None, :] * e[:, None]), HEAD_DIM, axis=1)
    gq = jnp.stack([per_lane(idx + 1.0), per_lane(c - idx)])
    gk = jnp.stack([per_lane(c - 1.0 - idx), per_lane(idx)])
    g_chunk = jnp.repeat(jnp.exp(log_g * c), HEAD_DIM).reshape(3, LANES)
    gc = jnp.broadcast_to(g_chunk[:, :, None], (3, LANES, LANES))
    blk = jnp.arange(LANES) // HEAD_DIM
    bd = (blk[:, None] == blk[None, :]).astype(F32)
    seg = jnp.arange(MXU_DIM) // HEAD_DIM
    avg = ((seg[:, None] == seg[None, :]).astype(F32) / HEAD_DIM).astype(BF16)
    return {"decay": decay, "gq": gq, "gk": gk, "gc": gc, "bd": bd}, avg


def _powers(br, bi, count):
    pr, pi = jnp.ones_like(br)[None], jnp.zeros_like(bi)[None]
    sr, si = br, bi
    while pr.shape[0] < count:
        nr, ni = _cmul(pr, pi, sr[None], si[None])
        pr, pi = jnp.concatenate([pr, nr]), jnp.concatenate([pi, ni])
        sr, si = _cmul(sr, si, sr, si)
    return pr[:count], pi[:count]


def _slab(re, im, axis):
    shp = re.shape
    split = shp[:axis] + (S5_SLABS, 1, LANES) + shp[axis + 1:]
    both = jnp.concatenate([re.reshape(split), im.reshape(split)], axis=axis + 1)
    return both.reshape(shp[:axis] + (2 * SSM_LANES,) + shp[axis + 1:])


def _s5_params(lam_re, lam_im, log_dt, b_re, b_im, c_re, c_im, d_skip):
    t = S5_CHUNK
    dt = jnp.exp(log_dt)[:, :, None]
    mag = jnp.exp(lam_re * dt)
    lbr, lbi = mag * jnp.cos(lam_im * dt), mag * jnp.sin(lam_im * dt)
    den = lam_re * lam_re + lam_im * lam_im
    cfr, cfi = _cmul(lbr - 1.0, lbi, lam_re / den, -lam_im / den)
    bbr, bbi = _cmul(cfr[..., None], cfi[..., None], b_re, b_im)
    eye = jnp.eye(SSM_GROUPS, dtype=F32)

    def b_block(m):
        return jnp.einsum('zgph,gk->zghkp', m, eye).reshape(2, SSM_WIDTH, SSM_LANES)

    def c_block(m):
        return jnp.einsum('zghp,gk->zgpkh', m, eye).reshape(2, SSM_LANES, SSM_WIDTH)

    bmat = _slab(b_block(bbr), b_block(bbi), 2).astype(BF16)
    cmat = _slab(c_block(c_re), -c_block(c_im), 1).astype(BF16)
    flat = lambda a: a.reshape(2, SSM_LANES)
    pr, pi = _powers(flat(lbr), flat(lbi), t)
    mag2 = lbr * lbr + lbi * lbi
    qr, qi = _powers(flat(lbr / mag2), flat(-lbi / mag2), t)
    pw = _slab(pr, pi, 2)
    iv = _slab(qr, qi, 2)
    lpow = jnp.stack([pw[:, 0], pw[::-1, 1]])
    linv = jnp.stack([iv[:, 0], iv[::-1, 1]])
    lam = _slab(flat(lbr), flat(lbi), 1)[:, None, :]
    tri = jnp.tril(jnp.ones((t, t), F32))
    tri = jnp.stack([tri, tri.T]).astype(BF16)
    return {"bmat": bmat, "cmat": cmat, "linv": linv, "lpow": lpow, "lam": lam, "tri": tri,
            "d": d_skip.reshape(1, SSM_WIDTH)}


def _trunk(x3, layers, ret_consts, avg):
    batch, seq_len, _ = x3.shape
    x = x3.reshape(batch * seq_len, D_MODEL)
    tabs = _rotary_tables(seq_len)
    for lp in layers:
        kvq, u, rqkv, rg = _in_proj(x, lp["w_in"], tabs, seq_len)
        rf, rb = _retention(rqkv, ret_consts, batch, seq_len)
        yf, yb = _s5(u, lp["s5"], batch, seq_len)
        x = _attn_mix(x, kvq, yf, yb, rf, rb, rg, lp, avg, batch, seq_len)
        x = _ffn(x, lp, seq_len)
    return x.reshape(batch, seq_len, D_MODEL)


def _prepare_layers(w_in, attn_sink, attn_out_g, lam_re, lam_im, log_dt, b_re, b_im, c_re, c_im, ssm_d, glu_w,
                    glu_b, ssm_out_g, w_out, ln1_g, ln1_b, w_up, conv_w, conv_b, w_down, ln2_g, ln2_b):
    depth = w_in.shape[0]
    heads = [h for c in range(3) for h in (c, 3 + c)]
    regroup = lambda a, axis: [lax.slice_in_dim(a, h * HEAD_DIM, (h + 1) * HEAD_DIM, axis=axis) for h in heads]
    w_in = jnp.concatenate([w_in[:, :, 384:640]] + regroup(w_in, 2) + [w_in[:, :, 640:]], axis=2).astype(BF16)
    w_out = jnp.concatenate(regroup(w_out, 1) + [w_out[:, ATTN_WIDTH:, :]], axis=1).astype(BF16)
    attn_out_g = jnp.concatenate(regroup(attn_out_g, 1), axis=1)
    layers = []
    for l in range(depth):
        layers.append({
            "w_in": w_in[l], "sink": attn_sink[l], "attn_g": attn_out_g[l].reshape(1, ATTN_WIDTH),
            "s5": _s5_params(lam_re[l], lam_im[l], log_dt[l], b_re[l], b_im[l], c_re[l], c_im[l], ssm_d[l]),
            "glu_w": glu_w[l].astype(BF16), "glu_b": glu_b[l].reshape(1, SSM_WIDTH),
            "ssm_g": ssm_out_g[l].reshape(1, SSM_WIDTH), "w_out": w_out[l],
            "ln1_g": ln1_g[l].reshape(1, D_MODEL), "ln1_b": ln1_b[l].reshape(1, D_MODEL),
            "w_up": w_up[l].astype(BF16), "conv_w": conv_w[l], "conv_b": conv_b[l].reshape(1, D_FF),
            "w_down": w_down[l].astype(BF16),
            "ln2_g": ln2_g[l].reshape(1, D_MODEL), "ln2_b": ln2_b[l].reshape(1, D_MODEL),
        })
    return layers


def kernel(x_prompt, x_sample, w_in, attn_sink, attn_out_g, ssm_lambda_re, ssm_lambda_im, ssm_log_dt, ssm_b_re, ssm_b_im, ssm_c_re, ssm_c_im, ssm_d, ssm_glu_w, ssm_glu_b, ssm_out_g, w_out, ln1_g, ln1_b, ffn_w_up, ffn_conv_w, ffn_conv_b, ffn_w_down, ln2_g, ln2_b):
    layers = _prepare_layers(w_in, attn_sink, attn_out_g, ssm_lambda_re, ssm_lambda_im, ssm_log_dt, ssm_b_re,
                             ssm_b_im, ssm_c_re, ssm_c_im, ssm_d, ssm_glu_w, ssm_glu_b, ssm_out_g, w_out, ln1_g,
                             ln1_b, ffn_w_up, ffn_conv_w, ffn_conv_b, ffn_w_down, ln2_g, ln2_b)
    ret_consts, avg = _retention_consts()
    return (_trunk(x_prompt, layers, ret_consts, avg), _trunk(x_sample, layers, ret_consts, avg))
```
